```python
import math
import jax, jax.numpy as jnp
from jax import lax
import numpy as np

D_MODEL = 1024
BATCH = 4
SEQ = 4096
DEPTH = 1
DEC_BATCH = 128
DEC_SEQ = 1
PAST_LEN = 8192
PAGE_SIZE = 128

HEAD_DIM = 64
D_MIX = D_MODEL
D_SB = D_MIX // 2
D_MOBA = D_MIX - D_SB
H_SB = D_SB // HEAD_DIM
H_MOBA = D_MOBA // HEAD_DIM
D_IN = 4 * D_SB + 4 * D_MOBA
SB_Q_BLOCK = 128
MOBA_BLOCK = 256
MOBA_TOPK = 3
MOBA_Q_CHUNK = 64
REL_BUCKETS = 32
REL_MAX_DIST = 128
RMS_EPS = 1e-6

kernel_name = "hymba_stickbreak_moba_decode_step"


def rmsnorm(x, g):
    x32 = x.astype(jnp.float32)
    y = x32 * lax.rsqrt(jnp.mean(x32 * x32, axis=-1, keepdims=True) + RMS_EPS)
    return (y * g.astype(jnp.float32)).astype(x.dtype)


def project(h, w):
    u = jnp.einsum('bsd,de->bse', h, w)
    B, S, _ = u.shape
    sizes = (D_SB,) * 4 + (D_MOBA,) * 4
    points = np.cumsum(sizes)[:-1].tolist()
    q_a, k_a, v_a, g_a, q_b, k_b, v_b, g_b = jnp.split(u, points, axis=-1)
    ha = lambda t: t.reshape(B, S, H_SB, HEAD_DIM)
    hb = lambda t: t.reshape(B, S, H_MOBA, HEAD_DIM)
    return ha(q_a), ha(k_a), ha(v_a), g_a, hb(q_b), hb(k_b), hb(v_b), g_b


def mix_out(x, o_a, g_a, o_b, g_b, w_out, g_post):
    B, S = x.shape[:2]
    y = jnp.concatenate([o_a.reshape(B, S, D_SB) * jax.nn.silu(g_a),
                         o_b.reshape(B, S, D_MOBA) * jax.nn.silu(g_b)], axis=-1)
    y = jnp.einsum('bse,ed->bsd', y, w_out)
    return x + rmsnorm(y, g_post)


def sb_weights(z, mask):
    log_keep = jnp.where(mask, jax.nn.log_sigmoid(-z), 0.0)
    after = lax.cumsum(log_keep, axis=z.ndim - 1, reverse=True) - log_keep
    log_a = jax.nn.log_sigmoid(z) + after
    return jnp.where(mask, jnp.exp(log_a), 0.0)


def sb_prompt(q, k, v):
    B, S, H, HD = q.shape
    nq = S // SB_Q_BLOCK
    scale = HEAD_DIM ** -0.5
    kpos = jnp.arange(S, dtype=jnp.int32)
    qb = q.reshape(B, nq, SB_Q_BLOCK, H, HD).transpose(1, 0, 2, 3, 4)

    def one(args):
        q_blk, i = args
        z = jnp.einsum('bqhd,bkhd->bhqk', q_blk, k, preferred_element_type=jnp.float32) * scale
        qpos = i * SB_Q_BLOCK + jnp.arange(SB_Q_BLOCK, dtype=jnp.int32)
        a = sb_weights(z, kpos[None, :] < qpos[:, None])
        return jnp.einsum('bhqk,bkhd->bqhd', a.astype(v.dtype), v)

    o = lax.map(one, (qb, jnp.arange(nq, dtype=jnp.int32)))
    return o.transpose(1, 0, 2, 3, 4).reshape(B, S, H, HD)


def sb_sample(q, k_new, v_new, k_past, v_past):
    Sd = q.shape[1]
    P = k_past.shape[1]
    scale = HEAD_DIM ** -0.5
    z = jnp.concatenate([
        jnp.einsum('bqhd,bkhd->bhqk', q, k_past, preferred_element_type=jnp.float32),
        jnp.einsum('bqhd,bkhd->bhqk', q, k_new, preferred_element_type=jnp.float32)], axis=-1) * scale
    idx = jnp.arange(Sd)
    mask = jnp.concatenate([jnp.ones((Sd, P), bool), idx[None, :] < idx[:, None]], axis=-1)
    a = sb_weights(z, mask).astype(v_new.dtype)
    return (jnp.einsum('bhqk,bkhd->bqhd', a[..., :P], v_past)
            + jnp.einsum('bhqk,bkhd->bqhd', a[..., P:], v_new))


def rel_bucket(n):
    n = jnp.maximum(n, 0)
    max_exact = REL_BUCKETS // 2
    nf = jnp.maximum(n, 1).astype(jnp.float32)
    large = max_exact + (jnp.log(nf / max_exact) / math.log(REL_MAX_DIST / max_exact)
                         * (REL_BUCKETS - max_exact)).astype(jnp.int32)
    large = jnp.minimum(large, REL_BUCKETS - 1)
    return jnp.where(n < max_exact, n, large)


def moba_blocks(k, v):
    B, T, H, HD = k.shape
    nb = -(-T // MOBA_BLOCK)
    pad = nb * MOBA_BLOCK - T

    def blk(t):
        t = jnp.pad(t, ((0, 0), (0, pad), (0, 0), (0, 0)))
        return t.reshape(B, nb, MOBA_BLOCK, H, HD).transpose(0, 3, 1, 2, 4)

    kb, vb = blk(k), blk(v)
    k_mean = jnp.mean(kb.astype(jnp.float32), axis=3)
    return kb, vb, k_mean


def moba_chunk(q_c, pos_c, kb, vb, k_mean, bias_table):
    Bq, Tc, H, HD = q_c.shape
    nb = kb.shape[2]
    k_sel = max(1, min(MOBA_TOPK, nb - 1))
    cur = pos_c // MOBA_BLOCK
    gate = jnp.einsum('bqhd,bhnd->bqhn', q_c.astype(jnp.float32), k_mean)
    past_ok = jnp.arange(nb)[None, :] < cur[:, None]
    gate = jnp.where(past_ok[None, :, None, :], gate, -jnp.inf)
    _, top = lax.top_k(gate, k_sel)
    own = jnp.broadcast_to(cur[None, :, None, None], (Bq, Tc, H, 1)).astype(top.dtype)
    blocks = jnp.concatenate([top, own], axis=-1)
    blk_ok = jnp.concatenate([jnp.arange(k_sel)[None, :] < cur[:, None],
                              jnp.ones((Tc, 1), bool)], axis=-1)
    bi = jnp.arange(Bq)[:, None, None, None]
    hi = jnp.arange(H)[None, None, :, None]
    kg = kb[bi, hi, blocks]
    vg = vb[bi, hi, blocks]
    kpos = blocks[..., None] * MOBA_BLOCK + jnp.arange(MOBA_BLOCK, dtype=blocks.dtype)
    rel = pos_c[None, :, None, None, None] - kpos
    mask = blk_ok[None, :, None, :, None] & (rel >= 0)
    bias = bias_table[rel_bucket(rel), hi[..., None]]
    logits = (jnp.einsum('bqhd,bqhnkd->bqhnk', q_c, kg, preferred_element_type=jnp.float32)
              * (HEAD_DIM ** -0.5) + bias.astype(jnp.float32))
    logits = jnp.where(mask, logits, -jnp.inf)
    p = jax.nn.softmax(logits.reshape(Bq, Tc, H, -1), axis=-1).reshape(logits.shape)
    return jnp.einsum('bqhnk,bqhnkd->bqhd', p.astype(vg.dtype), vg)


def moba_prompt(q, k, v, bias_table):
    B, S, H, HD = q.shape
    kb, vb, km = moba_blocks(k, v)
    nq = S // MOBA_Q_CHUNK
    qc = q.reshape(B, nq, MOBA_Q_CHUNK, H, HD).transpose(1, 0, 2, 3, 4)
    pc = jnp.arange(S, dtype=jnp.int32).reshape(nq, MOBA_Q_CHUNK)
    o = lax.map(lambda a: moba_chunk(a[0], a[1], kb, vb, km, bias_table), (qc, pc))
    return o.transpose(1, 0, 2, 3, 4).reshape(B, S, H, HD)


def gather_pages(pool, page_table):
    g = pool[page_table]
    Bd, NP, PS, H, HD = g.shape
    return g.reshape(Bd, NP * PS, H, HD)


def setup_inputs(seed: int = 0) -> dict:
    key = jax.random.key(seed)
    ks = jax.random.split(key, 12)
    n_pages = PAST_LEN // PAGE_SIZE
    n_used = DEC_BATCH * n_pages
    n_pool = n_used + max(1, n_used // 4)
    f32 = jnp.float32
    x_prompt = jax.random.normal(ks[0], (BATCH, SEQ, D_MODEL), f32)
    x_sample = jax.random.normal(ks[1], (DEC_BATCH, DEC_SEQ, D_MODEL), f32)
    cache_k_sb = jax.random.normal(ks[2], (DEPTH, n_pool, PAGE_SIZE, H_SB, HEAD_DIM), f32)
    cache_v_sb = jax.random.normal(ks[3], (DEPTH, n_pool, PAGE_SIZE, H_SB, HEAD_DIM), f32)
    cache_k_moba = jax.random.normal(ks[4], (DEPTH, n_pool, PAGE_SIZE, H_MOBA, HEAD_DIM), f32)
    cache_v_moba = jax.random.normal(ks[5], (DEPTH, n_pool, PAGE_SIZE, H_MOBA, HEAD_DIM), f32)
    page_table = jax.random.permutation(ks[6], n_pool)[:n_used].reshape(DEC_BATCH, n_pages).astype(jnp.int32)
    w_in = jax.random.normal(ks[7], (DEPTH, D_MODEL, D_IN), f32) * D_MODEL ** -0.5
    w_out = jax.random.normal(ks[8], (DEPTH, D_MIX, D_MODEL), f32) * D_MIX ** -0.5
    g_pre = 1.0 + 0.05 * jax.random.normal(ks[9], (DEPTH, D_MODEL), f32)
    g_post = 1.0 + 0.05 * jax.random.normal(ks[10], (DEPTH, D_MODEL), f32)
    rel_bias = 0.5 * jax.random.normal(ks[11], (REL_BUCKETS, H_MOBA), f32)
    return {"x_prompt": x_prompt, "x_sample": x_sample,
            "cache_k_sb": cache_k_sb, "cache_v_sb": cache_v_sb,
            "cache_k_moba": cache_k_moba, "cache_v_moba": cache_v_moba,
            "page_table": page_table, "w_in": w_in, "w_out": w_out,
            "g_pre": g_pre, "g_post": g_post, "rel_bias": rel_bias}


def reference(x_prompt, x_sample, cache_k_sb, cache_v_sb, cache_k_moba, cache_v_moba,
              page_table, w_in, w_out, g_pre, g_post, rel_bias):
    past_len = page_table.shape[1] * cache_k_sb.shape[2]
    dec_seq = x_sample.shape[1]
    pos_s = past_len + jnp.arange(dec_seq, dtype=jnp.int32)
    hp, hs = x_prompt, x_sample
    pk_a, pv_a, pk_b, pv_b = [], [], [], []
    sk_a, sv_a, sk_b, sv_b = [], [], [], []
    for l in range(DEPTH):
        q_a, k_a, v_a, g_a, q_b, k_b, v_b, g_b = project(rmsnorm(hp, g_pre[l]), w_in[l])
        o_a = sb_prompt(q_a, k_a, v_a)
        o_b = moba_prompt(q_b, k_b, v_b, rel_bias)
        hp = mix_out(hp, o_a, g_a, o_b, g_b, w_out[l], g_post[l])
        pk_a.append(k_a); pv_a.append(v_a); pk_b.append(k_b); pv_b.append(v_b)
        q_a, k_a, v_a, g_a, q_b, k_b, v_b, g_b = project(rmsnorm(hs, g_pre[l]), w_in[l])
        kpa = gather_pages(cache_k_sb[l], page_table)
        vpa = gather_pages(cache_v_sb[l], page_table)
        o_a = sb_sample(q_a, k_a, v_a, kpa, vpa)
        k_all = jnp.concatenate([gather_pages(cache_k_moba[l], page_table), k_b], axis=1)
        v_all = jnp.concatenate([gather_pages(cache_v_moba[l], page_table), v_b], axis=1)
        kb, vb, km = moba_blocks(k_all, v_all)
        o_b = moba_chunk(q_b, pos_s, kb, vb, km, rel_bias)
        hs = mix_out(hs, o_a, g_a, o_b, g_b, w_out[l], g_post[l])
        sk_a.append(k_a); sv_a.append(v_a); sk_b.append(k_b); sv_b.append(v_b)
    new_k_sb_prompt = jnp.stack(pk_a)
    new_v_sb_prompt = jnp.stack(pv_a)
    new_k_moba_prompt = jnp.stack(pk_b)
    new_v_moba_prompt = jnp.stack(pv_b)
    new_k_sb_sample = jnp.stack(sk_a)
    new_v_sb_sample = jnp.stack(sv_a)
    new_k_moba_sample = jnp.stack(sk_b)
    new_v_moba_sample = jnp.stack(sv_b)
    return (hp, hs, new_k_sb_prompt, new_v_sb_prompt, new_k_moba_prompt, new_v_moba_prompt,
            new_k_sb_sample, new_v_sb_sample, new_k_moba_sample, new_v_moba_sample)
```

```python
import functools
import math

import jax
import jax.numpy as jnp
from jax import lax
from jax.experimental import pallas as pl
from jax.experimental.pallas import tpu as pltpu

F32 = jnp.float32
BF16 = jnp.bfloat16
I32 = jnp.int32

HEAD_DIM = 64
LANES = 128
HEADS_PER_TILE = LANES // HEAD_DIM
GROUP_WIDTH = 512
N_HEADS = GROUP_WIDTH // HEAD_DIM
SB_BLOCK = 128
MOBA_BLOCK = 256
MOBA_TOPK = 3
REL_BUCKETS = 32
REL_MAX_DIST = 128
RMS_EPS = 1e-6
SCALE = HEAD_DIM ** -0.5
SB_EXIT = -110.0
VMEM_LIMIT = 56 * 1024 * 1024

_NT = (((1,), (1,)), ((), ()))


def _softplus(z):
    return jnp.maximum(z, 0.0) + jnp.log1p(jnp.exp(-jnp.abs(z)))


def _split_bf16(x):
    hi = x.astype(BF16)
    lo = (x - hi.astype(F32)).astype(BF16)
    return hi, lo


def _rel_bias(rel, table):
    max_exact = REL_BUCKETS // 2
    n = jnp.maximum(rel, 0)
    nf = jnp.maximum(n, 1).astype(F32)
    large = max_exact + (jnp.log(nf / max_exact) / math.log(REL_MAX_DIST / max_exact)
                         * (REL_BUCKETS - max_exact)).astype(I32)
    large = jnp.minimum(large, REL_BUCKETS - 1)
    bucket = jnp.where(n < max_exact, n, large)
    out = jnp.full(rel.shape, table(REL_BUCKETS - 1), F32)
    for k in range(REL_BUCKETS - 1):
        out = jnp.where(bucket == k, table(k), out)
    return out


def _proj_kernel(x_ref, g_ref, w_ref, *outs, prompt):
    x = x_ref[...]
    ms = jnp.mean(x * x, axis=-1, keepdims=True)
    h = (x * lax.rsqrt(ms + RMS_EPS) * g_ref[...]).astype(BF16)

    def group(c):
        return jnp.dot(h, w_ref[:, c * GROUP_WIDTH:(c + 1) * GROUP_WIDTH], preferred_element_type=F32)

    q_a, g_a, q_b, g_b, kt_a, vt_a, kt_b, vt_b = outs[:8]
    q_a[...] = group(0).astype(BF16)
    ka = group(1)
    va_t = group(2).T
    g_a[...] = group(3)
    q_b[...] = group(4).astype(BF16)
    kb = group(5)
    vb = group(6)
    vb_t = vb.T
    g_b[...] = group(7)
    kt_a[0] = ka.T
    vt_a[0] = va_t
    kt_b[0] = kb.T
    vt_b[0] = vb_t
    if prompt:
        ka16, kb16, vta16, vtb16, kmean = outs[8:]
        ka16[...] = ka.astype(BF16)
        kb16[...] = kb.astype(BF16)
        va16 = va_t.astype(BF16)
        for c in range(vta16.shape[0]):
            vta16[c] = va16[:, c * SB_BLOCK:(c + 1) * SB_BLOCK]
        vtb16[0] = vb_t.astype(BF16)
        kmean[0] = jnp.mean(kb, axis=0, keepdims=True)
    else:
        k_b, v_b = outs[8:]
        k_b[...] = kb
        v_b[...] = vb


def _project(x2d, g_pre, w16, prompt, seq):
    m, d = x2d.shape
    n_groups = w16.shape[1] // GROUP_WIDTH
    assert n_groups == 8
    tm = MOBA_BLOCK if prompt else m
    assert m % seq == 0 and seq % tm == 0
    per_seq = seq // tm
    row = lambda i: (i, 0)
    wide = pl.BlockSpec((tm, GROUP_WIDTH), row)
    tall = pl.BlockSpec((1, GROUP_WIDTH, tm), lambda i: (i // per_seq, 0, i % per_seq))
    wide_shape = lambda dt: jax.ShapeDtypeStruct((m, GROUP_WIDTH), dt)
    tall_shape = jax.ShapeDtypeStruct((m // seq, GROUP_WIDTH, seq), F32)
    out_shape = [wide_shape(BF16), wide_shape(F32), wide_shape(BF16), wide_shape(F32)] + [tall_shape] * 4
    out_specs = [wide] * 4 + [tall] * 4
    if prompt:
        per = tm // SB_BLOCK
        out_shape += [
            wide_shape(BF16), wide_shape(BF16),
            jax.ShapeDtypeStruct((m // SB_BLOCK, GROUP_WIDTH, SB_BLOCK), BF16),
            jax.ShapeDtypeStruct((m // tm, GROUP_WIDTH, tm), BF16),
            jax.ShapeDtypeStruct((m // tm, 1, GROUP_WIDTH), F32),
        ]
        out_specs += [
            wide, wide,
            pl.BlockSpec((per, GROUP_WIDTH, SB_BLOCK), lambda i: (i, 0, 0)),
            pl.BlockSpec((1, GROUP_WIDTH, tm), lambda i: (i, 0, 0)),
            pl.BlockSpec((1, 1, GROUP_WIDTH), lambda i: (i, 0, 0)),
        ]
    else:
        out_shape += [wide_shape(F32), wide_shape(F32)]
        out_specs += [wide, wide]
    return pl.pallas_call(
        functools.partial(_proj_kernel, prompt=prompt),
        grid=(m // tm,),
        in_specs=[pl.BlockSpec((tm, d), row),
                  pl.BlockSpec((1, d), lambda i: (0, 0)),
                  pl.BlockSpec(w16.shape, lambda i: (0, 0))],
        out_specs=out_specs,
        out_shape=out_shape,
        compiler_params=pltpu.CompilerParams(dimension_semantics=("arbitrary",),
                                             vmem_limit_bytes=VMEM_LIMIT),
        name="proj_prompt" if prompt else "proj_sample",
    )(x2d, g_pre, w16)


def _head_masked(q, scale=None):
    lane = lax.broadcasted_iota(I32, q.shape, 1)
    q32 = q.astype(F32) if scale is None else q.astype(F32) * scale
    return [jnp.where(lane // HEAD_DIM == hh, q32, 0.0).astype(q.dtype) for hh in range(HEADS_PER_TILE)]


def _sb_prompt_kernel(q_ref, k_ref, vt_ref, o_ref):
    t = SB_BLOCK
    i = pl.program_id(2)
    qs = _head_masked(q_ref[0], SCALE)
    key = lax.broadcasted_iota(I32, (t, t), 0)
    qry = lax.broadcasted_iota(I32, (t, t), 1)
    causal = key < qry
    upper = (qry > key).astype(BF16)
    upper2 = jnp.concatenate([upper, upper], axis=1)

    def tile(j, carry, acc, diag):
        start = pl.multiple_of(j * t, t)
        kt = k_ref[0, pl.ds(start, t), :]
        new_carry, new_acc = [], []
        for hh in range(HEADS_PER_TILE):
            z = lax.dot_general(kt, qs[hh], _NT, preferred_element_type=F32)
            sp = _softplus(z)
            lk = -sp
            if diag:
                lk = jnp.where(causal, lk, 0.0)
            hi, lo = _split_bf16(lk)
            after = carry[hh] + jnp.dot(upper2, jnp.concatenate([hi, lo], axis=0),
                                        preferred_element_type=F32)
            a = jnp.exp((z - sp) + after)
            if diag:
                a = jnp.where(causal, a, 0.0)
            vt = vt_ref[j, hh * HEAD_DIM:(hh + 1) * HEAD_DIM, :]
            new_acc.append(acc[hh] + jnp.dot(vt, a.astype(BF16), preferred_element_type=F32))
            new_carry.append(carry[hh] + jnp.sum(lk, axis=0, keepdims=True))
        return tuple(new_carry), tuple(new_acc)

    zeros_c = tuple(jnp.zeros((1, t), F32) for _ in range(HEADS_PER_TILE))
    zeros_a = tuple(jnp.zeros((HEAD_DIM, t), F32) for _ in range(HEADS_PER_TILE))
    carry, acc = tile(i, zeros_c, zeros_a, True)

    def cond(s):
        j, carry, _ = s
        live = jnp.max(jnp.maximum(carry[0], carry[1])) > SB_EXIT
        return jnp.logical_and(j >= 0, live)

    def body(s):
        j, carry, acc = s
        carry, acc = tile(j, carry, acc, False)
        return j - 1, carry, acc

    _, _, acc = lax.while_loop(cond, body, (i - 1, carry, acc))
    o_ref[0] = jnp.concatenate(acc, axis=0).T


def _sb_prompt(q16, k16, vt, batch, seq):
    n_blk = seq // SB_BLOCK
    return pl.pallas_call(
        _sb_prompt_kernel,
        grid=(batch, N_HEADS // HEADS_PER_TILE, n_blk),
        in_specs=[pl.BlockSpec((1, SB_BLOCK, LANES), lambda b, h, i: (b, i, h)),
                  pl.BlockSpec((1, seq, LANES), lambda b, h, i: (b, 0, h)),
                  pl.BlockSpec((n_blk, LANES, SB_BLOCK), lambda b, h, i: (b, h, 0))],
        out_specs=pl.BlockSpec((1, SB_BLOCK, LANES), lambda b, h, i: (b, i, h)),
        out_shape=jax.ShapeDtypeStruct((batch, seq, GROUP_WIDTH), F32),
        compiler_params=pltpu.CompilerParams(
            dimension_semantics=("arbitrary", "arbitrary", "arbitrary"), vmem_limit_bytes=VMEM_LIMIT),
        name="sb_prompt",
    )(q16, k16, vt)


def _bias_tile_kernel(table_ref, o_ref):
    h = pl.program_id(0)
    t = MOBA_BLOCK
    key = lax.broadcasted_iota(I32, (t, t), 0)
    qry = lax.broadcasted_iota(I32, (t, t), 1)
    for c in range(2):
        o_ref[0, c] = _rel_bias(c * t + qry - key, lambda k: table_ref[k, h])


def _bias_tiles(rel_bias):
    n_heads = rel_bias.shape[1]
    return pl.pallas_call(
        _bias_tile_kernel,
        grid=(n_heads,),
        in_specs=[pl.BlockSpec(memory_space=pltpu.SMEM)],
        out_specs=pl.BlockSpec((1, 2, MOBA_BLOCK, MOBA_BLOCK), lambda h: (h, 0, 0, 0)),
        out_shape=jax.ShapeDtypeStruct((n_heads, 2, MOBA_BLOCK, MOBA_BLOCK), F32),
        name="moba_bias_tiles",
    )(rel_bias)


def _moba_prompt_kernel(table_ref, q_ref, k_ref, vt_ref, kmean_ref, bias_ref, o_ref, sel_ref):
    t = MOBA_BLOCK
    i = pl.program_id(2)
    hp = pl.program_id(1)
    n_blk = kmean_ref.shape[1]
    q = q_ref[0]
    qm = _head_masked(q)
    qs = _head_masked(q, SCALE)
    km = kmean_ref[0].astype(BF16)
    key = lax.broadcasted_iota(I32, (t, t), 0)
    qry = lax.broadcasted_iota(I32, (t, t), 1)
    blk = lax.broadcasted_iota(I32, (n_blk, t), 0)
    outs = []
    for hh in range(HEADS_PER_TILE):
        gate = lax.dot_general(km, qm[hh], _NT, preferred_element_type=F32)
        rank = jnp.zeros((n_blk, t), I32)
        for m in range(n_blk):
            gm = gate[m:m + 1, :]
            beats = jnp.logical_or(gm > gate, jnp.logical_and(gm == gate, m < blk))
            rank = rank + jnp.where(beats, (m < i).astype(I32), 0)
        sel = jnp.logical_and(rank < MOBA_TOPK, blk < i)
        sel_ref[...] = sel.astype(F32)
        far_bias = table_ref[REL_BUCKETS - 1, hp * HEADS_PER_TILE + hh]

        def scores(n, bias):
            start = pl.multiple_of(n * t, t)
            kt = k_ref[0, pl.ds(start, t), :]
            return lax.dot_general(kt, qs[hh], _NT, preferred_element_type=F32) + bias

        def update(state, n, s):
            m_i, l_i, acc = state
            m_new = jnp.maximum(m_i, jnp.max(s, axis=0, keepdims=True))
            alpha = jnp.exp(m_i - m_new)
            p = jnp.exp(s - m_new)
            vt = vt_ref[n, hh * HEAD_DIM:(hh + 1) * HEAD_DIM, :]
            acc = alpha * acc + jnp.dot(vt, p.astype(BF16), preferred_element_type=F32)
            return m_new, alpha * l_i + jnp.sum(p, axis=0, keepdims=True), acc

        s = jnp.where(key <= qry, scores(i, bias_ref[hh, 0]), -jnp.inf)
        m_i = jnp.max(s, axis=0, keepdims=True)
        p = jnp.exp(s - m_i)
        vt = vt_ref[i, hh * HEAD_DIM:(hh + 1) * HEAD_DIM, :]
        state = (m_i, jnp.sum(p, axis=0, keepdims=True),
                 jnp.dot(vt, p.astype(BF16), preferred_element_type=F32))

        def near(state):
            n = i - 1
            chosen = sel_ref[pl.ds(n, 1), :] > 0.0
            s = jnp.where(chosen, scores(n, bias_ref[hh, 1]), -jnp.inf)
            return update(state, n, s)

        state = lax.cond(i >= 1, near, lambda st: st, state)

        def far(n, state):
            chosen = sel_ref[pl.ds(n, 1), :] > 0.0
            s = jnp.where(chosen, scores(n, far_bias), -jnp.inf)
            return update(state, n, s)

        m_i, l_i, acc = lax.fori_loop(0, jnp.maximum(i - 1, 0), far, state)
        outs.append(acc / l_i)
    o_ref[0] = jnp.concatenate(outs, axis=0).T


def _moba_prompt(rel_bias, q16, k16, vt, kmean, bias_tiles, batch, seq):
    n_blk = seq // MOBA_BLOCK
    assert n_blk > MOBA_TOPK
    grid_spec = pl.GridSpec(
        grid=(batch, N_HEADS // HEADS_PER_TILE, n_blk),
        in_specs=[pl.BlockSpec(memory_space=pltpu.SMEM),
                  pl.BlockSpec((1, MOBA_BLOCK, LANES), lambda b, h, i: (b, i, h)),
                  pl.BlockSpec((1, seq, LANES), lambda b, h, i: (b, 0, h)),
                  pl.BlockSpec((n_blk, LANES, MOBA_BLOCK), lambda b, h, i: (b, h, 0)),
                  pl.BlockSpec((1, n_blk, LANES), lambda b, h, i: (b, 0, h)),
                  pl.BlockSpec((HEADS_PER_TILE, 2, MOBA_BLOCK, MOBA_BLOCK), lambda b, h, i: (h, 0, 0, 0))],
        out_specs=pl.BlockSpec((1, MOBA_BLOCK, LANES), lambda b, h, i: (b, i, h)),
        scratch_shapes=[pltpu.VMEM((n_blk, MOBA_BLOCK), F32)],
    )
    return pl.pallas_call(
        _moba_prompt_kernel,
        grid_spec=grid_spec,
        out_shape=jax.ShapeDtypeStruct((batch, seq, GROUP_WIDTH), F32),
        compiler_params=pltpu.CompilerParams(
            dimension_semantics=("arbitrary", "arbitrary", "arbitrary"), vmem_limit_bytes=VMEM_LIMIT),
        name="moba_prompt",
    )(rel_bias, q16, k16, vt, kmean, bias_tiles)


def _mix_kernel(x_ref, oa_ref, ga_ref, ob_ref, gb_ref, w_ref, g_ref, y_ref):
    ga = ga_ref[...]
    gb = gb_ref[...]
    ya = (oa_ref[...] * (ga * jax.nn.sigmoid(ga))).astype(BF16)
    yb = (ob_ref[...] * (gb * jax.nn.sigmoid(gb))).astype(BF16)
    y = (jnp.dot(ya, w_ref[:GROUP_WIDTH, :], preferred_element_type=F32)
         + jnp.dot(yb, w_ref[GROUP_WIDTH:, :], preferred_element_type=F32))
    ms = jnp.mean(y * y, axis=-1, keepdims=True)
    y_ref[...] = x_ref[...] + y * lax.rsqrt(ms + RMS_EPS) * g_ref[...]


def _mix_out(x2d, o_a, g_a, o_b, g_b, w16, g_post):
    m, d = x2d.shape
    tm = min(m, 256)
    assert m % tm == 0
    row = lambda i: (i, 0)
    wide = pl.BlockSpec((tm, GROUP_WIDTH), row)
    return pl.pallas_call(
        _mix_kernel,
        grid=(m // tm,),
        in_specs=[pl.BlockSpec((tm, d), row), wide, wide, wide, wide,
                  pl.BlockSpec(w16.shape, lambda i: (0, 0)),
                  pl.BlockSpec((1, d), lambda i: (0, 0))],
        out_specs=pl.BlockSpec((tm, d), row),
        out_shape=jax.ShapeDtypeStruct((m, d), F32),
        compiler_params=pltpu.CompilerParams(dimension_semantics=("arbitrary",),
                                             vmem_limit_bytes=VMEM_LIMIT),
        name="mix_out",
    )(x2d, o_a, g_a, o_b, g_b, w16, g_post)


def _block_diag_rows(row):
    shape = (N_HEADS, GROUP_WIDTH)
    own = lax.broadcasted_iota(I32, shape, 1) // HEAD_DIM == lax.broadcasted_iota(I32, shape, 0)
    return jnp.where(own, jnp.broadcast_to(row.astype(F32), shape), 0.0).astype(row.dtype), own


def _sb_sample_kernel(pt_ref, q_ref, k1_ref, k2_ref, v1_ref, v2_ref, kpool_ref, vpool_ref, o_ref,
                      kbuf, vbuf, sem, *, n_pages):
    b = pl.program_id(0)
    page = kbuf.shape[1]
    qbd, own = _block_diag_rows(q_ref[0] * SCALE)
    lower = (lax.broadcasted_iota(I32, (page, page), 0)
             > lax.broadcasted_iota(I32, (page, page), 1)).astype(BF16)
    lower2 = jnp.concatenate([lower, lower], axis=0)

    def step(kp, vp, carry, acc):
        z = jnp.dot(qbd, kp.astype(BF16), preferred_element_type=F32)
        sp = _softplus(z)
        lk = -sp
        hi, lo = _split_bf16(lk)
        after = carry + jnp.dot(jnp.concatenate([hi, lo], axis=1), lower2, preferred_element_type=F32)
        a = jnp.exp((z - sp) + after)
        acc = acc + lax.dot_general(a.astype(BF16), vp.astype(BF16), _NT,
                                    preferred_element_type=F32)
        return carry + jnp.sum(lk, axis=1, keepdims=True), acc

    carry = jnp.zeros((N_HEADS, 1), F32)
    acc = jnp.zeros((N_HEADS, GROUP_WIDTH), F32)
    carry, acc = step(k1_ref[0], v1_ref[0], carry, acc)
    carry, acc = step(k2_ref[0], v2_ref[0], carry, acc)

    def cond(s):
        p, carry, _ = s
        return jnp.logical_and(p >= 0, jnp.max(carry) > SB_EXIT)

    def body(s):
        p, carry, acc = s
        phys = pt_ref[b * n_pages + p]
        ck = pltpu.make_async_copy(kpool_ref.at[phys], kbuf, sem.at[0])
        cv = pltpu.make_async_copy(vpool_ref.at[phys], vbuf, sem.at[1])
        ck.start()
        cv.start()
        ck.wait()
        cv.wait()
        carry, acc = step(kbuf[...], vbuf[...], carry, acc)
        return p - 1, carry, acc

    _, _, acc = lax.while_loop(cond, body, (n_pages - 3, carry, acc))
    o_ref[0] = jnp.sum(jnp.where(own, acc, 0.0), axis=0, keepdims=True)


def _sb_sample(pt_flat, q16, kpool, vpool, n_seq, n_pages):
    page = kpool.shape[2]
    assert n_pages >= 2
    last = lambda off: (lambda b, pt: (pt[b * n_pages + n_pages - off], 0, 0))
    page_spec = lambda off: pl.BlockSpec((1, GROUP_WIDTH, page), last(off))
    grid_spec = pltpu.PrefetchScalarGridSpec(
        num_scalar_prefetch=1,
        grid=(n_seq,),
        in_specs=[pl.BlockSpec((1, 1, GROUP_WIDTH), lambda b, pt: (b, 0, 0)),
                  page_spec(1), page_spec(2), page_spec(1), page_spec(2),
                  pl.BlockSpec(memory_space=pl.ANY), pl.BlockSpec(memory_space=pl.ANY)],
        out_specs=pl.BlockSpec((1, 1, GROUP_WIDTH), lambda b, pt: (b, 0, 0)),
        scratch_shapes=[pltpu.VMEM((GROUP_WIDTH, page), F32), pltpu.VMEM((GROUP_WIDTH, page), F32),
                        pltpu.SemaphoreType.DMA((2,))],
    )
    return pl.pallas_call(
        functools.partial(_sb_sample_kernel, n_pages=n_pages),
        grid_spec=grid_spec,
        out_shape=jax.ShapeDtypeStruct((n_seq, 1, GROUP_WIDTH), F32),
        compiler_params=pltpu.CompilerParams(dimension_semantics=("arbitrary",),
                                             vmem_limit_bytes=VMEM_LIMIT),
        name="sb_sample",
    )(pt_flat, q16, kpool, kpool, vpool, vpool, kpool, vpool)


def _moba_gate_kernel(pt_ref, q_ref, kpool_ref, top_ref, buf, sem, ksum, *, n_pages, n_slots):
    b = pl.program_id(0)
    n_seq = pl.num_programs(0)
    page = buf.shape[2]
    pages_per_block = MOBA_BLOCK // page
    n_blk = n_pages // pages_per_block
    assert n_blk <= LANES

    def copy(g):
        slot = g % n_slots
        return pltpu.make_async_copy(kpool_ref.at[pt_ref[g]], buf.at[slot], sem.at[slot])

    @pl.when(b == 0)
    def _():
        for g in range(n_slots):
            copy(g).start()

    col = lax.broadcasted_iota(I32, (GROUP_WIDTH, LANES), 1)
    ksum[...] = jnp.zeros(ksum.shape, F32)

    def block_sum(n, _):
        total = jnp.zeros((GROUP_WIDTH, page), F32)
        for c in range(pages_per_block):
            g = b * n_pages + n * pages_per_block + c
            copy(g).wait()
            total = total + buf[g % n_slots]

            @pl.when(g + n_slots < n_seq * n_pages)
            def _():
                copy(g + n_slots).start()
        ksum[...] = jnp.where(col == n, jnp.sum(total, axis=1, keepdims=True), ksum[...])
        return 0

    lax.fori_loop(0, n_blk, block_sum, 0)

    kmean = (ksum[...] * (1.0 / MOBA_BLOCK)).astype(BF16)
    qbd, _ = _block_diag_rows(q_ref[0])
    gate = jnp.dot(qbd, kmean, preferred_element_type=F32)
    blk = lax.broadcasted_iota(I32, gate.shape, 1)
    rank = jnp.zeros(gate.shape, I32)
    for m in range(n_blk):
        gm = gate[:, m:m + 1]
        beats = jnp.logical_or(gm > gate, jnp.logical_and(gm == gate, m < blk))
        rank = rank + jnp.where(beats, 1, 0)
    top = jnp.zeros((N_HEADS, LANES), I32)
    for j in range(MOBA_TOPK):
        hit = jnp.logical_and(rank == j, blk < n_blk)
        idx = jnp.sum(jnp.where(hit, blk, 0), axis=1, keepdims=True)
        top = jnp.where(blk == j, idx, top)
    top_ref[0] = top


def _moba_gate(pt_flat, q16, kpool, n_seq, n_pages):
    page = kpool.shape[2]
    assert MOBA_BLOCK % page == 0 and n_pages % (MOBA_BLOCK // page) == 0
    assert n_pages // (MOBA_BLOCK // page) >= MOBA_TOPK
    n_slots = min(32, n_seq * n_pages)
    grid_spec = pltpu.PrefetchScalarGridSpec(
        num_scalar_prefetch=1,
        grid=(n_seq,),
        in_specs=[pl.BlockSpec((1, 1, GROUP_WIDTH), lambda b, pt: (b, 0, 0)),
                  pl.BlockSpec(memory_space=pl.ANY)],
        out_specs=pl.BlockSpec((1, N_HEADS, LANES), lambda b, pt: (b, 0, 0)),
        scratch_shapes=[pltpu.VMEM((n_slots, GROUP_WIDTH, page), F32),
                        pltpu.SemaphoreType.DMA((n_slots,)),
                        pltpu.VMEM((GROUP_WIDTH, LANES), F32)],
    )
    return pl.pallas_call(
        functools.partial(_moba_gate_kernel, n_pages=n_pages, n_slots=n_slots),
        grid_spec=grid_spec,
        out_shape=jax.ShapeDtypeStruct((n_seq, N_HEADS, LANES), I32),
        compiler_params=pltpu.CompilerParams(dimension_semantics=("arbitrary",),
                                             vmem_limit_bytes=VMEM_LIMIT),
        name="moba_gate",
    )(pt_flat, q16, kpool)


def _moba_sample_kernel(pt_ref, top_ref, table_ref, q_ref, knew_ref, vnew_ref, kpool_ref, vpool_ref,
                        o_ref, kbuf, vbuf, sem, *, n_pages):
    b = pl.program_id(0)
    n_seq = pl.num_programs(0)
    page = kpool_ref.shape[3]
    pages_per_block = MOBA_BLOCK // page
    n_sel = MOBA_TOPK * MOBA_BLOCK
    past_len = n_pages * page

    def copies(seq, slot):
        out = []
        for h in range(N_HEADS):
            for j in range(MOBA_TOPK):
                blk = top_ref[(seq * N_HEADS + h) * MOBA_TOPK + j]
                for c in range(pages_per_block):
                    phys = pt_ref[seq * n_pages + blk * pages_per_block + c]
                    cols = pl.ds((j * pages_per_block + c) * page, page)
                    out.append(pltpu.make_async_copy(kpool_ref.at[phys, h],
                                                     kbuf.at[slot, h, :, cols], sem.at[slot, 0]))
                    out.append(pltpu.make_async_copy(vpool_ref.at[phys, h],
                                                     vbuf.at[slot, h, :, cols], sem.at[slot, 1]))
        return out

    slot = b % 2

    @pl.when(b == 0)
    def _():
        for c in copies(0, 0):
            c.start()

    @pl.when(b + 1 < n_seq)
    def _():
        for c in copies(b + 1, 1 - slot):
            c.start()

    for c in copies(b, slot):
        c.wait()

    pos = lax.broadcasted_iota(I32, (1, n_sel), 1)
    which = pos // MOBA_BLOCK
    for h in range(N_HEADS):
        q_f = q_ref[0, h:h + 1, :].astype(F32)
        q8 = jnp.broadcast_to(q_f * SCALE, (8, HEAD_DIM)).astype(BF16)
        kh = kbuf[slot, h].astype(BF16)
        s = jnp.dot(q8, kh, preferred_element_type=F32)[0:1]
        blk = jnp.zeros((1, n_sel), I32)
        for j in range(MOBA_TOPK):
            blk = jnp.where(which == j, top_ref[(b * N_HEADS + h) * MOBA_TOPK + j], blk)
        rel = past_len - (blk * MOBA_BLOCK + pos % MOBA_BLOCK)
        s = s + _rel_bias(rel, lambda k: table_ref[k, h])
        knew = knew_ref[0, h:h + 1, :].astype(BF16).astype(F32)
        own = jnp.sum(q_f * knew, axis=1, keepdims=True) * SCALE + table_ref[0, h]
        m = jnp.maximum(jnp.max(s, axis=1, keepdims=True), own)
        p = jnp.exp(s - m)
        p_own = jnp.exp(own - m)
        denom = jnp.sum(p, axis=1, keepdims=True) + p_own
        vh = vbuf[slot, h].astype(BF16)
        pv = lax.dot_general(jnp.broadcast_to(p, (8, n_sel)).astype(BF16), vh, _NT,
                             preferred_element_type=F32)[0:1]
        vnew = vnew_ref[0, h:h + 1, :].astype(BF16).astype(F32)
        o_ref[0, h:h + 1, :] = (pv + p_own.astype(BF16).astype(F32) * vnew) / denom


def _moba_sample(pt_flat, top_flat, rel_bias, q16, k_new, v_new, kpool, vpool, n_seq, n_pages):
    n_sel = MOBA_TOPK * MOBA_BLOCK
    head_rows = pl.BlockSpec((1, N_HEADS, HEAD_DIM), lambda b, pt, top: (b, 0, 0))
    grid_spec = pltpu.PrefetchScalarGridSpec(
        num_scalar_prefetch=2,
        grid=(n_seq,),
        in_specs=[pl.BlockSpec(memory_space=pltpu.SMEM),
                  head_rows, head_rows, head_rows,
                  pl.BlockSpec(memory_space=pl.ANY), pl.BlockSpec(memory_space=pl.ANY)],
        out_specs=head_rows,
        scratch_shapes=[pltpu.VMEM((2, N_HEADS, HEAD_DIM, n_sel), F32),
                        pltpu.VMEM((2, N_HEADS, HEAD_DIM, n_sel), F32),
                        pltpu.SemaphoreType.DMA((2, 2))],
    )
    return pl.pallas_call(
        functools.partial(_moba_sample_kernel, n_pages=n_pages),
        grid_spec=grid_spec,
        out_shape=jax.ShapeDtypeStruct((n_seq, N_HEADS, HEAD_DIM), F32),
        compiler_params=pltpu.CompilerParams(dimension_semantics=("arbitrary",),
                                             vmem_limit_bytes=VMEM_LIMIT),
        name="moba_sample",
    )(pt_flat, top_flat, rel_bias, q16, k_new, v_new, kpool, vpool)


def kernel(x_prompt, x_sample, cache_k_sb, cache_v_sb, cache_k_moba, cache_v_moba, page_table,
           w_in, w_out, g_pre, g_post, rel_bias):
    depth, d_model, d_in = w_in.shape
    assert depth == 1 and d_in == 8 * GROUP_WIDTH and w_out.shape[1] == 2 * GROUP_WIDTH
    batch, seq, _ = x_prompt.shape
    n_seq, dec_seq, _ = x_sample.shape
    assert dec_seq == 1 and seq % MOBA_BLOCK == 0
    n_pool, page = cache_k_sb.shape[1:3]
    n_pages = page_table.shape[1]

    w_in16 = w_in[0].astype(BF16)
    w_out16 = w_out[0].astype(BF16)
    pt_flat = page_table.reshape(-1).astype(I32)
    pool4 = lambda c: jnp.transpose(c[0], (0, 2, 3, 1))
    pool3 = lambda c: pool4(c).reshape(n_pool, GROUP_WIDTH, page)

    (q_a, g_a, q_b, g_b, kt_a, vt_a, kt_b, vt_b, ka16, kb16, vta16, vtb16, kmean_b) = _project(
        x_prompt.reshape(batch * seq, d_model), g_pre, w_in16, True, seq)
    as_seq = lambda a: a.reshape(batch, seq, GROUP_WIDTH)
    o_a = _sb_prompt(as_seq(q_a), as_seq(ka16), vta16, batch, seq)
    o_b = _moba_prompt(rel_bias, as_seq(q_b), as_seq(kb16), vtb16,
                       kmean_b.reshape(batch, seq // MOBA_BLOCK, GROUP_WIDTH),
                       _bias_tiles(rel_bias), batch, seq)
    y_prompt = _mix_out(x_prompt.reshape(batch * seq, d_model), o_a.reshape(batch * seq, GROUP_WIDTH),
                        g_a, o_b.reshape(batch * seq, GROUP_WIDTH), g_b, w_out16, g_post)

    sq_a, sg_a, sq_b, sg_b, skt_a, svt_a, skt_b, svt_b, sk_b, sv_b = _project(
        x_sample.reshape(n_seq, d_model), g_pre, w_in16, False, n_seq)
    so_a = _sb_sample(pt_flat, sq_a.reshape(n_seq, 1, GROUP_WIDTH), pool3(cache_k_sb), pool3(cache_v_sb),
                      n_seq, n_pages)
    top = _moba_gate(pt_flat, sq_b.reshape(n_seq, 1, GROUP_WIDTH), pool3(cache_k_moba), n_seq, n_pages)
    heads = lambda a: a.reshape(n_seq, N_HEADS, HEAD_DIM)
    so_b = _moba_sample(pt_flat, top[:, :, :MOBA_TOPK].reshape(-1), rel_bias, heads(sq_b), heads(sk_b),
                        heads(sv_b), pool4(cache_k_moba), pool4(cache_v_moba), n_seq, n_pages)
    y_sample = _mix_out(x_sample.reshape(n_seq, d_model), so_a.reshape(n_seq, GROUP_WIDTH), sg_a,
                        so_b.reshape(n_seq, GROUP_WIDTH), sg_b, w_out16, g_post)

    kv = lambda a: jnp.transpose(a.reshape(a.shape[0], N_HEADS, HEAD_DIM, a.shape[2]), (0, 3, 1, 2))[None]
    kv_s = lambda a: jnp.transpose(kv(a), (0, 2, 1, 3, 4))
    return (y_prompt.reshape(batch, seq, d_model), y_sample.reshape(n_seq, 1, d_model),
            kv(kt_a), kv(vt_a), kv(kt_b), kv(vt_b),
            kv_s(skt_a), kv_s(svt_a), kv_s(skt_b), kv_s(svt_b))
```

```python
import functools
import math

import jax
import jax.numpy as jnp
from jax import lax
from jax.experimental import pallas as pl
from jax.experimental.pallas import tpu as pltpu

F32 = jnp.float32
BF16 = jnp.bfloat16
I32 = jnp.int32

HEAD_DIM = 64
LANES = 128
HEADS_PER_TILE = LANES // HEAD_DIM
GROUP_WIDTH = 512
N_HEADS = GROUP_WIDTH // HEAD_DIM
N_PAIRS = N_HEADS // HEADS_PER_TILE
SB_BLOCK = 128
MOBA_BLOCK = 256
MOBA_TOPK = 3
ONES_ROWS = 16
REL_BUCKETS = 32
REL_MAX_DIST = 128
RMS_EPS = 1e-6
SCALE = HEAD_DIM ** -0.5
SB_EXIT = -110.0
VMEM_LIMIT = 56 * 1024 * 1024

_NT = (((1,), (1,)), ((), ()))


def _softplus(z):
    return jnp.maximum(z, 0.0) + jnp.log1p(jnp.exp(-jnp.abs(z)))


def _split_bf16(x):
    hi = x.astype(BF16)
    lo = (x - hi.astype(F32)).astype(BF16)
    return hi, lo


def _rel_bias(rel, table):
    max_exact = REL_BUCKETS // 2
    n = jnp.maximum(rel, 0)
    nf = jnp.maximum(n, 1).astype(F32)
    large = max_exact + (jnp.log(nf / max_exact) / math.log(REL_MAX_DIST / max_exact)
                         * (REL_BUCKETS - max_exact)).astype(I32)
    large = jnp.minimum(large, REL_BUCKETS - 1)
    bucket = jnp.where(n < max_exact, n, large)
    out = jnp.broadcast_to(table(REL_BUCKETS - 1), rel.shape).astype(F32)
    for k in range(REL_BUCKETS - 1):
        out = jnp.where(bucket == k, table(k), out)
    return out


def _silu(g):
    return g * jax.nn.sigmoid(g)


def _store_masked_heads(q, qs_ref, scale):
    rows = q.shape[0]
    lane = lax.broadcasted_iota(I32, (rows, LANES), 1)
    for pair in range(N_PAIRS):
        q32 = q[:, pair * LANES:(pair + 1) * LANES].astype(F32) * scale
        for hh in range(HEADS_PER_TILE):
            qs_ref[pair * HEADS_PER_TILE + hh] = jnp.where(lane // HEAD_DIM == hh, q32, 0.0).astype(BF16)


def _proj_kernel(x_ref, g_ref, w_ref, *outs, prompt):
    x = x_ref[...]
    ms = jnp.mean(x * x, axis=-1, keepdims=True)
    h = (x * lax.rsqrt(ms + RMS_EPS) * g_ref[...]).astype(BF16)

    def group(c):
        return jnp.dot(h, w_ref[:, c * GROUP_WIDTH:(c + 1) * GROUP_WIDTH], preferred_element_type=F32)

    q_a, g_a, q_b, g_b, kt_a, vt_a, kt_b, vt_b = outs[:8]
    q_a[...] = group(0).astype(BF16)
    ka = group(1)
    va_t = group(2).T
    g_a[...] = group(3)
    q_b[...] = group(4).astype(BF16)
    kb = group(5)
    vb = group(6)
    vb_t = vb.T
    g_b[...] = group(7)
    kt_a[0] = ka.T
    vt_a[0] = va_t
    kt_b[0] = kb.T
    vt_b[0] = vb_t
    if prompt:
        ka16, kb16, vta16, vtb16, kmean = outs[8:]
        ka16[...] = ka.astype(BF16)
        kb16[...] = kb.astype(BF16)
        va16 = va_t.astype(BF16)
        for c in range(vta16.shape[0]):
            vta16[c] = va16[:, c * SB_BLOCK:(c + 1) * SB_BLOCK]
        vtb16[0] = vb_t.astype(BF16)
        kmean[0] = jnp.mean(kb, axis=0, keepdims=True)
    else:
        k_b, v_b = outs[8:]
        k_b[...] = kb
        v_b[...] = vb


def _project(x2d, g_pre, w16, prompt, seq):
    m, d = x2d.shape
    n_groups = w16.shape[1] // GROUP_WIDTH
    assert n_groups == 8
    tm = MOBA_BLOCK if prompt else m
    assert m % seq == 0 and seq % tm == 0
    per_seq = seq // tm
    row = lambda i: (i, 0)
    wide = pl.BlockSpec((tm, GROUP_WIDTH), row)
    tall = pl.BlockSpec((1, GROUP_WIDTH, tm), lambda i: (i // per_seq, 0, i % per_seq))
    wide_shape = lambda dt: jax.ShapeDtypeStruct((m, GROUP_WIDTH), dt)
    tall_shape = jax.ShapeDtypeStruct((m // seq, GROUP_WIDTH, seq), F32)
    out_shape = [wide_shape(BF16), wide_shape(F32), wide_shape(BF16), wide_shape(F32)] + [tall_shape] * 4
    out_specs = [wide] * 4 + [tall] * 4
    if prompt:
        per = tm // SB_BLOCK
        out_shape += [
            wide_shape(BF16), wide_shape(BF16),
            jax.ShapeDtypeStruct((m // SB_BLOCK, GROUP_WIDTH, SB_BLOCK), BF16),
            jax.ShapeDtypeStruct((m // tm, GROUP_WIDTH, tm), BF16),
            jax.ShapeDtypeStruct((m // tm, 1, GROUP_WIDTH), F32),
        ]
        out_specs += [
            wide, wide,
            pl.BlockSpec((per, GROUP_WIDTH, SB_BLOCK), lambda i: (i, 0, 0)),
            pl.BlockSpec((1, GROUP_WIDTH, tm), lambda i: (i, 0, 0)),
            pl.BlockSpec((1, 1, GROUP_WIDTH), lambda i: (i, 0, 0)),
        ]
    else:
        out_shape += [wide_shape(F32), wide_shape(F32)]
        out_specs += [wide, wide]
    return pl.pallas_call(
        functools.partial(_proj_kernel, prompt=prompt),
        grid=(m // tm,),
        in_specs=[pl.BlockSpec((tm, d), row),
                  pl.BlockSpec((1, d), lambda i: (0, 0)),
                  pl.BlockSpec(w16.shape, lambda i: (0, 0))],
        out_specs=out_specs,
        out_shape=out_shape,
        compiler_params=pltpu.CompilerParams(dimension_semantics=("arbitrary",),
                                             vmem_limit_bytes=VMEM_LIMIT),
        name="proj_prompt" if prompt else "proj_sample",
    )(x2d, g_pre, w16)


def _sb_prompt_kernel(q_ref, k_ref, vt_ref, g_ref, y_ref, qs_ref, acc_ref, z_ref, cs_ref):
    t = SB_BLOCK
    i = pl.program_id(1)
    _store_masked_heads(q_ref[0], qs_ref, SCALE)
    key = lax.broadcasted_iota(I32, (t, t), 0)
    qry = lax.broadcasted_iota(I32, (t, t), 1)
    causal = key < qry
    upper = (qry > key).astype(BF16)
    upper2 = jnp.concatenate([upper, upper], axis=1)

    def tile(j, carry, diag):
        start = pl.multiple_of(j * t, t)
        for h in range(N_HEADS):
            pair = h // HEADS_PER_TILE
            kt = k_ref[0, pl.ds(start, t), pair * LANES:(pair + 1) * LANES]
            z_ref[h] = lax.dot_general(kt, qs_ref[h], _NT, preferred_element_type=F32)
        new_carry = []
        for h in range(N_HEADS):
            z = z_ref[h]
            sp = _softplus(z)
            lk = -sp
            if diag:
                lk = jnp.where(causal, lk, 0.0)
            z_ref[h] = z - sp
            hi, lo = _split_bf16(lk)
            cs_ref[h] = jnp.dot(upper2, jnp.concatenate([hi, lo], axis=0), preferred_element_type=F32)
            new_carry.append(carry[h] + jnp.sum(lk, axis=0, keepdims=True))
        for h in range(N_HEADS):
            a = jnp.exp(z_ref[h] + (cs_ref[h] + carry[h]))
            if diag:
                a = jnp.where(causal, a, 0.0)
            vt = vt_ref[j, h * HEAD_DIM:(h + 1) * HEAD_DIM, :]
            pv = jnp.dot(vt, a.astype(BF16), preferred_element_type=F32)
            rows = pl.ds(h * HEAD_DIM, HEAD_DIM)
            if diag:
                acc_ref[rows, :] = pv
            else:
                acc_ref[rows, :] += pv
        return tuple(new_carry)

    carry = tile(i, tuple(jnp.zeros((1, t), F32) for _ in range(N_HEADS)), True)

    def cond(s):
        j, carry = s
        top = carry[0]
        for c in carry[1:]:
            top = jnp.maximum(top, c)
        return jnp.logical_and(j >= 0, jnp.max(top) > SB_EXIT)

    def body(s):
        j, carry = s
        return j - 1, tile(j, carry, False)

    lax.while_loop(cond, body, (i - 1, carry))
    y_ref[0] = (acc_ref[...].T * _silu(g_ref[0])).astype(BF16)


def _sb_prompt(q16, k16, vt16, gate, batch, seq):
    n_blk = seq // SB_BLOCK
    tile_spec = pl.BlockSpec((1, SB_BLOCK, GROUP_WIDTH), lambda b, i: (b, i, 0))
    grid_spec = pl.GridSpec(
        grid=(batch, n_blk),
        in_specs=[tile_spec,
                  pl.BlockSpec((1, seq, GROUP_WIDTH), lambda b, i: (b, 0, 0)),
                  pl.BlockSpec((n_blk, GROUP_WIDTH, SB_BLOCK), lambda b, i: (b, 0, 0)),
                  tile_spec],
        out_specs=tile_spec,
        scratch_shapes=[pltpu.VMEM((N_HEADS, SB_BLOCK, LANES), BF16),
                        pltpu.VMEM((GROUP_WIDTH, SB_BLOCK), F32),
                        pltpu.VMEM((N_HEADS, SB_BLOCK, SB_BLOCK), F32),
                        pltpu.VMEM((N_HEADS, SB_BLOCK, SB_BLOCK), F32)],
    )
    return pl.pallas_call(
        _sb_prompt_kernel,
        grid_spec=grid_spec,
        out_shape=jax.ShapeDtypeStruct((batch, seq, GROUP_WIDTH), BF16),
        compiler_params=pltpu.CompilerParams(
            dimension_semantics=("arbitrary", "arbitrary"), vmem_limit_bytes=VMEM_LIMIT),
        name="sb_prompt",
    )(q16, k16, vt16, gate)


def _bias_tile_kernel(table_ref, o_ref):
    h = pl.program_id(0)
    t = MOBA_BLOCK
    key = lax.broadcasted_iota(I32, (t, t), 0)
    qry = lax.broadcasted_iota(I32, (t, t), 1)
    for c in range(2):
        o_ref[0, c] = _rel_bias(c * t + qry - key, lambda k: table_ref[k, h])


def _bias_tiles(rel_bias):
    n_heads = rel_bias.shape[1]
    return pl.pallas_call(
        _bias_tile_kernel,
        grid=(n_heads,),
        in_specs=[pl.BlockSpec(memory_space=pltpu.SMEM)],
        out_specs=pl.BlockSpec((1, 2, MOBA_BLOCK, MOBA_BLOCK), lambda h: (h, 0, 0, 0)),
        out_shape=jax.ShapeDtypeStruct((n_heads, 2, MOBA_BLOCK, MOBA_BLOCK), F32),
        name="moba_bias_tiles",
    )(rel_bias)


def _moba_prompt_kernel(table_ref, q_ref, k_ref, vt_ref, kmean_ref, bias_ref, g_ref, y_ref,
                        qs_ref, mask_ref, m_ref, acc_ref, s_ref):
    t = MOBA_BLOCK
    i = pl.program_id(1)
    n_blk = kmean_ref.shape[1]
    q = q_ref[0]
    _store_masked_heads(q, qs_ref, SCALE)
    key = lax.broadcasted_iota(I32, (t, t), 0)
    qry = lax.broadcasted_iota(I32, (t, t), 1)
    blk = lax.broadcasted_iota(I32, (n_blk, t), 0)
    lane = lax.broadcasted_iota(I32, (n_blk, LANES), 1)

    for h in range(N_HEADS):
        pair, hh = divmod(h, HEADS_PER_TILE)
        km = kmean_ref[0, :, pair * LANES:(pair + 1) * LANES]
        km = jnp.where(lane // HEAD_DIM == hh, km, 0.0).astype(BF16)
        gate = lax.dot_general(km, q[:, pair * LANES:(pair + 1) * LANES], _NT,
                               preferred_element_type=F32)
        rank = jnp.zeros((n_blk, t), I32)
        for m in range(n_blk):
            gm = gate[m:m + 1, :]
            beats = jnp.logical_or(gm > gate, jnp.logical_and(gm == gate, m < blk))
            rank = rank + jnp.where(beats, (m < i).astype(I32), 0)
        sel = jnp.logical_and(rank < MOBA_TOPK, blk < i)
        mask_ref[h] = jnp.where(sel, 0.0, -jnp.inf)

    ones = jnp.ones((ONES_ROWS, t), BF16)

    def tile(n, bias, shift, first):
        start = pl.multiple_of(n * t, t)
        alphas, offsets = [], []
        for h in range(N_HEADS):
            pair = h // HEADS_PER_TILE
            row = pl.ds(h, 1)
            kt = k_ref[0, pl.ds(start, t), pair * LANES:(pair + 1) * LANES]
            s = lax.dot_general(kt, qs_ref[h], _NT, preferred_element_type=F32)
            if bias is not None:
                s = s + bias(h)
            s_ref[h] = s
            m_tile = jnp.max(s, axis=0, keepdims=True)
            if shift is not None:
                m_tile = m_tile + shift(h)
            if first:
                m_new = m_tile
            else:
                m_old = m_ref[row, :]
                m_new = jnp.maximum(m_old, m_tile)
                alphas.append(jnp.exp(m_old - m_new))
            m_ref[row, :] = m_new
            offsets.append(m_new if shift is None else m_new - shift(h))
        for h in range(N_HEADS):
            p = jnp.exp(s_ref[h] - offsets[h]).astype(BF16)
            vt = jnp.concatenate([vt_ref[n, pl.ds(h * HEAD_DIM, HEAD_DIM), :], ones], axis=0)
            pv = jnp.dot(vt, p, preferred_element_type=F32)
            acc_ref[h] = pv if first else alphas[h] * acc_ref[h] + pv

    tile(i, lambda h: jnp.where(key <= qry, bias_ref[h, 0], -jnp.inf), None, True)

    @pl.when(i >= 1)
    def _():
        n = i - 1
        tile(n, lambda h: bias_ref[h, 1], lambda h: mask_ref[h, pl.ds(n, 1), :], False)

    def far(n, _):
        tile(n, None, lambda h: mask_ref[h, pl.ds(n, 1), :] + table_ref[REL_BUCKETS - 1, h], False)
        return 0

    lax.fori_loop(0, jnp.maximum(i - 1, 0), far, 0)

    outs = []
    for h in range(N_HEADS):
        acc = acc_ref[h]
        outs.append(acc[:HEAD_DIM] / acc[HEAD_DIM:HEAD_DIM + 1])
    y_ref[0] = (jnp.concatenate(outs, axis=0).T * _silu(g_ref[0])).astype(BF16)


def _moba_prompt(rel_bias, q16, k16, vt16, kmean, bias_tiles, gate, batch, seq):
    n_blk = seq // MOBA_BLOCK
    assert n_blk > MOBA_TOPK
    tile_spec = pl.BlockSpec((1, MOBA_BLOCK, GROUP_WIDTH), lambda b, i: (b, i, 0))
    grid_spec = pl.GridSpec(
        grid=(batch, n_blk),
        in_specs=[pl.BlockSpec(memory_space=pltpu.SMEM),
                  tile_spec,
                  pl.BlockSpec((1, seq, GROUP_WIDTH), lambda b, i: (b, 0, 0)),
                  pl.BlockSpec((n_blk, GROUP_WIDTH, MOBA_BLOCK), lambda b, i: (b, 0, 0)),
                  pl.BlockSpec((1, n_blk, GROUP_WIDTH), lambda b, i: (b, 0, 0)),
                  pl.BlockSpec((N_HEADS, 2, MOBA_BLOCK, MOBA_BLOCK), lambda b, i: (0, 0, 0, 0)),
                  tile_spec],
        out_specs=tile_spec,
        scratch_shapes=[pltpu.VMEM((N_HEADS, MOBA_BLOCK, LANES), BF16),
                        pltpu.VMEM((N_HEADS, n_blk, MOBA_BLOCK), F32),
                        pltpu.VMEM((N_HEADS, MOBA_BLOCK), F32),
                        pltpu.VMEM((N_HEADS, HEAD_DIM + ONES_ROWS, MOBA_BLOCK), F32),
                        pltpu.VMEM((N_HEADS, MOBA_BLOCK, MOBA_BLOCK), F32)],
    )
    return pl.pallas_call(
        _moba_prompt_kernel,
        grid_spec=grid_spec,
        out_shape=jax.ShapeDtypeStruct((batch, seq, GROUP_WIDTH), BF16),
        compiler_params=pltpu.CompilerParams(
            dimension_semantics=("arbitrary", "arbitrary"), vmem_limit_bytes=VMEM_LIMIT),
        name="moba_prompt",
    )(rel_bias, q16, k16, vt16, kmean, bias_tiles, gate)


def _mix_kernel(x_ref, a_ref, b_ref, *rest, gated):
    if gated:
        w_ref, g_ref, y_ref = rest
        ya, yb = a_ref[...], b_ref[...]
    else:
        ga_ref, gb_ref, w_ref, g_ref, y_ref = rest
        ya = (a_ref[...] * _silu(ga_ref[...])).astype(BF16)
        yb = (b_ref[...] * _silu(gb_ref[...])).astype(BF16)
    y = (jnp.dot(ya, w_ref[:GROUP_WIDTH, :], preferred_element_type=F32)
         + jnp.dot(yb, w_ref[GROUP_WIDTH:, :], preferred_element_type=F32))
    ms = jnp.mean(y * y, axis=-1, keepdims=True)
    y_ref[...] = x_ref[...] + y * lax.rsqrt(ms + RMS_EPS) * g_ref[...]


def _mix_out(x2d, a, b, gates, w16, g_post):
    m, d = x2d.shape
    tm = min(m, 256)
    assert m % tm == 0
    row = lambda i: (i, 0)
    wide = pl.BlockSpec((tm, GROUP_WIDTH), row)
    extra = () if gates is None else tuple(gates)
    return pl.pallas_call(
        functools.partial(_mix_kernel, gated=gates is None),
        grid=(m // tm,),
        in_specs=[pl.BlockSpec((tm, d), row), wide, wide] + [wide] * len(extra)
                 + [pl.BlockSpec(w16.shape, lambda i: (0, 0)), pl.BlockSpec((1, d), lambda i: (0, 0))],
        out_specs=pl.BlockSpec((tm, d), row),
        out_shape=jax.ShapeDtypeStruct((m, d), F32),
        compiler_params=pltpu.CompilerParams(dimension_semantics=("arbitrary",),
                                             vmem_limit_bytes=VMEM_LIMIT),
        name="mix_out",
    )(x2d, a, b, *extra, w16, g_post)


def _block_diag_rows(row):
    shape = (N_HEADS, GROUP_WIDTH)
    own = lax.broadcasted_iota(I32, shape, 1) // HEAD_DIM == lax.broadcasted_iota(I32, shape, 0)
    return jnp.where(own, jnp.broadcast_to(row.astype(F32), shape), 0.0).astype(row.dtype), own


def _sb_sample_kernel(pt_ref, q_ref, k1_ref, k2_ref, v1_ref, v2_ref, kpool_ref, vpool_ref, o_ref,
                      kbuf, vbuf, sem, *, n_pages):
    b = pl.program_id(0)
    page = kbuf.shape[1]
    qbd, own = _block_diag_rows(q_ref[0] * SCALE)
    lower = (lax.broadcasted_iota(I32, (page, page), 0)
             > lax.broadcasted_iota(I32, (page, page), 1)).astype(BF16)
    lower2 = jnp.concatenate([lower, lower], axis=0)

    def step(kp, vp, carry, acc):
        z = jnp.dot(qbd, kp.astype(BF16), preferred_element_type=F32)
        sp = _softplus(z)
        lk = -sp
        hi, lo = _split_bf16(lk)
        after = carry + jnp.dot(jnp.concatenate([hi, lo], axis=1), lower2, preferred_element_type=F32)
        a = jnp.exp((z - sp) + after)
        acc = acc + lax.dot_general(a.astype(BF16), vp.astype(BF16), _NT,
                                    preferred_element_type=F32)
        return carry + jnp.sum(lk, axis=1, keepdims=True), acc

    carry = jnp.zeros((N_HEADS, 1), F32)
    acc = jnp.zeros((N_HEADS, GROUP_WIDTH), F32)
    carry, acc = step(k1_ref[0], v1_ref[0], carry, acc)
    carry, acc = step(k2_ref[0], v2_ref[0], carry, acc)

    def cond(s):
        p, carry, _ = s
        return jnp.logical_and(p >= 0, jnp.max(carry) > SB_EXIT)

    def body(s):
        p, carry, acc = s
        phys = pt_ref[b * n_pages + p]
        ck = pltpu.make_async_copy(kpool_ref.at[phys], kbuf, sem.at[0])
        cv = pltpu.make_async_copy(vpool_ref.at[phys], vbuf, sem.at[1])
        ck.start()
        cv.start()
        ck.wait()
        cv.wait()
        carry, acc = step(kbuf[...], vbuf[...], carry, acc)
        return p - 1, carry, acc

    _, _, acc = lax.while_loop(cond, body, (n_pages - 3, carry, acc))
    o_ref[0] = jnp.sum(jnp.where(own, acc, 0.0), axis=0, keepdims=True)


def _sb_sample(pt_flat, q16, kpool, vpool, n_seq, n_pages):
    page = kpool.shape[2]
    assert n_pages >= 2
    last = lambda off: (lambda b, pt: (pt[b * n_pages + n_pages - off], 0, 0))
    page_spec = lambda off: pl.BlockSpec((1, GROUP_WIDTH, page), last(off))
    grid_spec = pltpu.PrefetchScalarGridSpec(
        num_scalar_prefetch=1,
        grid=(n_seq,),
        in_specs=[pl.BlockSpec((1, 1, GROUP_WIDTH), lambda b, pt: (b, 0, 0)),
                  page_spec(1), page_spec(2), page_spec(1), page_spec(2),
                  pl.BlockSpec(memory_space=pl.ANY), pl.BlockSpec(memory_space=pl.ANY)],
        out_specs=pl.BlockSpec((1, 1, GROUP_WIDTH), lambda b, pt: (b, 0, 0)),
        scratch_shapes=[pltpu.VMEM((GROUP_WIDTH, page), F32), pltpu.VMEM((GROUP_WIDTH, page), F32),
                        pltpu.SemaphoreType.DMA((2,))],
    )
    return pl.pallas_call(
        functools.partial(_sb_sample_kernel, n_pages=n_pages),
        grid_spec=grid_spec,
        out_shape=jax.ShapeDtypeStruct((n_seq, 1, GROUP_WIDTH), F32),
        compiler_params=pltpu.CompilerParams(dimension_semantics=("arbitrary",),
                                             vmem_limit_bytes=VMEM_LIMIT),
        name="sb_sample",
    )(pt_flat, q16, kpool, kpool, vpool, vpool, kpool, vpool)


def _moba_gate_kernel(pt_ref, q_ref, kpool_ref, top_ref, buf, sem, ksum, *, n_pages, n_slots):
    b = pl.program_id(0)
    n_seq = pl.num_programs(0)
    page = buf.shape[2]
    pages_per_block = MOBA_BLOCK // page
    n_blk = n_pages // pages_per_block
    assert n_blk <= LANES

    def copy(g):
        slot = g % n_slots
        return pltpu.make_async_copy(kpool_ref.at[pt_ref[g]], buf.at[slot], sem.at[slot])

    @pl.when(b == 0)
    def _():
        for g in range(n_slots):
            copy(g).start()

    col = lax.broadcasted_iota(I32, (GROUP_WIDTH, LANES), 1)
    ksum[...] = jnp.zeros(ksum.shape, F32)

    def block_sum(n, _):
        total = jnp.zeros((GROUP_WIDTH, page), F32)
        for c in range(pages_per_block):
            g = b * n_pages + n * pages_per_block + c
            copy(g).wait()
            total = total + buf[g % n_slots]

            @pl.when(g + n_slots < n_seq * n_pages)
            def _():
                copy(g + n_slots).start()
        ksum[...] = jnp.where(col == n, jnp.sum(total, axis=1, keepdims=True), ksum[...])
        return 0

    lax.fori_loop(0, n_blk, block_sum, 0)

    kmean = (ksum[...] * (1.0 / MOBA_BLOCK)).astype(BF16)
    qbd, _ = _block_diag_rows(q_ref[0])
    gate = jnp.dot(qbd, kmean, preferred_element_type=F32)
    blk = lax.broadcasted_iota(I32, gate.shape, 1)
    rank = jnp.zeros(gate.shape, I32)
    for m in range(n_blk):
        gm = gate[:, m:m + 1]
        beats = jnp.logical_or(gm > gate, jnp.logical_and(gm == gate, m < blk))
        rank = rank + jnp.where(beats, 1, 0)
    top = jnp.zeros((N_HEADS, LANES), I32)
    for j in range(MOBA_TOPK):
        hit = jnp.logical_and(rank == j, blk < n_blk)
        idx = jnp.sum(jnp.where(hit, blk, 0), axis=1, keepdims=True)
        top = jnp.where(blk == j, idx, top)
    top_ref[0] = top


def _moba_gate(pt_flat, q16, kpool, n_seq, n_pages):
    page = kpool.shape[2]
    assert MOBA_BLOCK % page == 0 and n_pages % (MOBA_BLOCK // page) == 0
    assert n_pages // (MOBA_BLOCK // page) >= MOBA_TOPK
    n_slots = min(32, n_seq * n_pages)
    grid_spec = pltpu.PrefetchScalarGridSpec(
        num_scalar_prefetch=1,
        grid=(n_seq,),
        in_specs=[pl.BlockSpec((1, 1, GROUP_WIDTH), lambda b, pt: (b, 0, 0)),
                  pl.BlockSpec(memory_space=pl.ANY)],
        out_specs=pl.BlockSpec((1, N_HEADS, LANES), lambda b, pt: (b, 0, 0)),
        scratch_shapes=[pltpu.VMEM((n_slots, GROUP_WIDTH, page), F32),
                        pltpu.SemaphoreType.DMA((n_slots,)),
                        pltpu.VMEM((GROUP_WIDTH, LANES), F32)],
    )
    return pl.pallas_call(
        functools.partial(_moba_gate_kernel, n_pages=n_pages, n_slots=n_slots),
        grid_spec=grid_spec,
        out_shape=jax.ShapeDtypeStruct((n_seq, N_HEADS, LANES), I32),
        compiler_params=pltpu.CompilerParams(dimension_semantics=("arbitrary",),
                                             vmem_limit_bytes=VMEM_LIMIT),
        name="moba_gate",
    )(pt_flat, q16, kpool)


def _moba_sample_kernel(pt_ref, top_ref, topv_ref, table_ref, q_ref, knew_ref, vnew_ref,
                        kpool_ref, vpool_ref, o_ref, kbuf, vbuf, sem, *, n_pages):
    b = pl.program_id(0)
    n_seq = pl.num_programs(0)
    page = kpool_ref.shape[3]
    pages_per_block = MOBA_BLOCK // page
    n_sel = MOBA_TOPK * MOBA_BLOCK
    past_len = n_pages * page

    def copies(seq, slot):
        out = []
        for h in range(N_HEADS):
            for j in range(MOBA_TOPK):
                blk = top_ref[(seq * N_HEADS + h) * MOBA_TOPK + j]
                for c in range(pages_per_block):
                    phys = pt_ref[seq * n_pages + blk * pages_per_block + c]
                    cols = pl.ds((j * pages_per_block + c) * page, page)
                    out.append(pltpu.make_async_copy(kpool_ref.at[phys, h],
                                                     kbuf.at[slot, h, :, cols], sem.at[slot, 0]))
                    out.append(pltpu.make_async_copy(vpool_ref.at[phys, h],
                                                     vbuf.at[slot, h, :, cols], sem.at[slot, 1]))
        return out

    slot = b % 2

    @pl.when(b == 0)
    def _():
        for c in copies(0, 0):
            c.start()

    @pl.when(b + 1 < n_seq)
    def _():
        for c in copies(b + 1, 1 - slot):
            c.start()

    for c in copies(b, slot):
        c.wait()

    q_f = q_ref[0].astype(F32)
    head = lax.broadcasted_iota(I32, (N_HEADS, n_sel), 0)
    s = jnp.zeros((N_HEADS, n_sel), F32)
    for h in range(N_HEADS):
        q8 = jnp.broadcast_to(q_f[h:h + 1, :] * SCALE, (8, HEAD_DIM)).astype(BF16)
        s_h = jnp.dot(q8, kbuf[slot, h].astype(BF16), preferred_element_type=F32)
        s = jnp.where(head == h, s_h, s)
    pos = lax.broadcasted_iota(I32, (N_HEADS, n_sel), 1)
    which = pos // MOBA_BLOCK
    topv = topv_ref[0]
    blk = jnp.zeros((N_HEADS, n_sel), I32)
    for j in range(MOBA_TOPK):
        blk = jnp.where(which == j, topv[:, j:j + 1], blk)
    rel = past_len - (blk * MOBA_BLOCK + pos % MOBA_BLOCK)
    s = s + _rel_bias(rel, lambda k: table_ref[:, k:k + 1])
    knew = knew_ref[0].astype(BF16).astype(F32)
    own = jnp.sum(q_f * knew, axis=1, keepdims=True) * SCALE + table_ref[:, 0:1]
    m = jnp.maximum(jnp.max(s, axis=1, keepdims=True), own)
    p = jnp.exp(s - m)
    p_own = jnp.exp(own - m)
    denom = jnp.sum(p, axis=1, keepdims=True) + p_own
    p16 = p.astype(BF16)
    head_o = lax.broadcasted_iota(I32, (N_HEADS, HEAD_DIM), 0)
    pv = jnp.zeros((N_HEADS, HEAD_DIM), F32)
    for h in range(N_HEADS):
        pv_h = lax.dot_general(p16, vbuf[slot, h].astype(BF16), _NT, preferred_element_type=F32)
        pv = jnp.where(head_o == h, pv_h, pv)
    vnew = vnew_ref[0].astype(BF16).astype(F32)
    o_ref[0] = (pv + p_own.astype(BF16).astype(F32) * vnew) / denom


def _moba_sample(pt_flat, top, rel_bias, q16, k_new, v_new, kpool, vpool, n_seq, n_pages):
    n_sel = MOBA_TOPK * MOBA_BLOCK
    head_rows = pl.BlockSpec((1, N_HEADS, HEAD_DIM), lambda b, pt, tp: (b, 0, 0))
    whole = lambda a: pl.BlockSpec(a.shape, lambda b, pt, tp: (0,) * a.ndim)
    table_t = rel_bias.T
    grid_spec = pltpu.PrefetchScalarGridSpec(
        num_scalar_prefetch=2,
        grid=(n_seq,),
        in_specs=[pl.BlockSpec((1, N_HEADS, LANES), lambda b, pt, tp: (b, 0, 0)),
                  whole(table_t),
                  head_rows, head_rows, head_rows,
                  pl.BlockSpec(memory_space=pl.ANY), pl.BlockSpec(memory_space=pl.ANY)],
        out_specs=head_rows,
        scratch_shapes=[pltpu.VMEM((2, N_HEADS, HEAD_DIM, n_sel), F32),
                        pltpu.VMEM((2, N_HEADS, HEAD_DIM, n_sel), F32),
                        pltpu.SemaphoreType.DMA((2, 2))],
    )
    return pl.pallas_call(
        functools.partial(_moba_sample_kernel, n_pages=n_pages),
        grid_spec=grid_spec,
        out_shape=jax.ShapeDtypeStruct((n_seq, N_HEADS, HEAD_DIM), F32),
        compiler_params=pltpu.CompilerParams(dimension_semantics=("arbitrary",),
                                             vmem_limit_bytes=VMEM_LIMIT),
        name="moba_sample",
    )(pt_flat, top[:, :, :MOBA_TOPK].reshape(-1), top, table_t, q16, k_new, v_new, kpool, vpool)


def kernel(x_prompt, x_sample, cache_k_sb, cache_v_sb, cache_k_moba, cache_v_moba, page_table,
           w_in, w_out, g_pre, g_post, rel_bias):
    depth, d_model, d_in = w_in.shape
    assert depth == 1 and d_in == 8 * GROUP_WIDTH and w_out.shape[1] == 2 * GROUP_WIDTH
    batch, seq, _ = x_prompt.shape
    n_seq, dec_seq, _ = x_sample.shape
    assert dec_seq == 1 and seq % MOBA_BLOCK == 0
    n_pool, page = cache_k_sb.shape[1:3]
    n_pages = page_table.shape[1]

    w_in16 = w_in[0].astype(BF16)
    w_out16 = w_out[0].astype(BF16)
    pt_flat = page_table.reshape(-1).astype(I32)
    pool4 = lambda c: jnp.transpose(c[0], (0, 2, 3, 1))
    pool3 = lambda c: pool4(c).reshape(n_pool, GROUP_WIDTH, page)

    (q_a, g_a, q_b, g_b, kt_a, vt_a, kt_b, vt_b, ka16, kb16, vta16, vtb16, kmean_b) = _project(
        x_prompt.reshape(batch * seq, d_model), g_pre, w_in16, True, seq)
    as_seq = lambda a: a.reshape(batch, seq, GROUP_WIDTH)
    y_a = _sb_prompt(as_seq(q_a), as_seq(ka16), vta16, as_seq(g_a), batch, seq)
    y_b = _moba_prompt(rel_bias, as_seq(q_b), as_seq(kb16), vtb16,
                       kmean_b.reshape(batch, seq // MOBA_BLOCK, GROUP_WIDTH),
                       _bias_tiles(rel_bias), as_seq(g_b), batch, seq)
    y_prompt = _mix_out(x_prompt.reshape(batch * seq, d_model), y_a.reshape(batch * seq, GROUP_WIDTH),
                        y_b.reshape(batch * seq, GROUP_WIDTH), None, w_out16, g_post)

    sq_a, sg_a, sq_b, sg_b, skt_a, svt_a, skt_b, svt_b, sk_b, sv_b = _project(
        x_sample.reshape(n_seq, d_model), g_pre, w_in16, False, n_seq)
    so_a = _sb_sample(pt_flat, sq_a.reshape(n_seq, 1, GROUP_WIDTH), pool3(cache_k_sb), pool3(cache_v_sb),
                      n_seq, n_pages)
    top = _moba_gate(pt_flat, sq_b.reshape(n_seq, 1, GROUP_WIDTH), pool3(cache_k_moba), n_seq, n_pages)
    heads = lambda a: a.reshape(n_seq, N_HEADS, HEAD_DIM)
    so_b = _moba_sample(pt_flat, top, rel_bias, heads(sq_b), heads(sk_b), heads(sv_b),
                        pool4(cache_k_moba), pool4(cache_v_moba), n_seq, n_pages)
    y_sample = _mix_out(x_sample.reshape(n_seq, d_model), so_a.reshape(n_seq, GROUP_WIDTH),
                        so_b.reshape(n_seq, GROUP_WIDTH), (sg_a, sg_b), w_out16, g_post)

    kv = lambda a: jnp.transpose(a.reshape(a.shape[0], N_HEADS, HEAD_DIM, a.shape[2]), (0, 3, 1, 2))[None]
    kv_s = lambda a: jnp.transpose(kv(a), (0, 2, 1, 3, 4))
    return (y_prompt.reshape(batch, seq, d_model), y_sample.reshape(n_seq, 1, d_model),
            kv(kt_a), kv(vt_a), kv(kt_b), kv(vt_b),
            kv_s(skt_a), kv_s(svt_a), kv_s(skt_b), kv_s(svt_b))
```

```python
import functools
import math

import jax
import jax.numpy as jnp
from jax import lax
from jax.experimental import pallas as pl
from jax.experimental.pallas import tpu as pltpu

F32 = jnp.float32
BF16 = jnp.bfloat16
I32 = jnp.int32

HEAD_DIM = 64
LANES = 128
HEADS_PER_TILE = LANES // HEAD_DIM
GROUP_WIDTH = 512
N_HEADS = GROUP_WIDTH // HEAD_DIM
N_PAIRS = N_HEADS // HEADS_PER_TILE
SB_BLOCK = 128
MOBA_BLOCK = 256
MOBA_TOPK = 3
ONES_ROWS = 16
REL_BUCKETS = 32
REL_MAX_DIST = 128
RMS_EPS = 1e-6
SCALE = HEAD_DIM ** -0.5
SB_EXIT = -110.0
VMEM_LIMIT = 56 * 1024 * 1024

_NT = (((1,), (1,)), ((), ()))


def _softplus(z):
    return jnp.maximum(z, 0.0) + jnp.log1p(jnp.exp(-jnp.abs(z)))


def _split_bf16(x):
    hi = x.astype(BF16)
    lo = (x - hi.astype(F32)).astype(BF16)
    return hi, lo


def _rel_bias(rel, table):
    max_exact = REL_BUCKETS // 2
    n = jnp.maximum(rel, 0)
    nf = jnp.maximum(n, 1).astype(F32)
    large = max_exact + (jnp.log(nf / max_exact) / math.log(REL_MAX_DIST / max_exact)
                         * (REL_BUCKETS - max_exact)).astype(I32)
    large = jnp.minimum(large, REL_BUCKETS - 1)
    bucket = jnp.where(n < max_exact, n, large)
    out = jnp.broadcast_to(table(REL_BUCKETS - 1), rel.shape).astype(F32)
    for k in range(REL_BUCKETS - 1):
        out = jnp.where(bucket == k, table(k), out)
    return out


def _silu(g):
    return g * jax.nn.sigmoid(g)


def _store_masked_heads(q, qs_ref, scale):
    rows = q.shape[0]
    lane = lax.broadcasted_iota(I32, (rows, LANES), 1)
    for pair in range(N_PAIRS):
        q32 = q[:, pair * LANES:(pair + 1) * LANES].astype(F32) * scale
        for hh in range(HEADS_PER_TILE):
            qs_ref[pair * HEADS_PER_TILE + hh] = jnp.where(lane // HEAD_DIM == hh, q32, 0.0).astype(BF16)


def _proj_kernel(x_ref, g_ref, w_ref, *outs, prompt):
    x = x_ref[...]
    ms = jnp.mean(x * x, axis=-1, keepdims=True)
    h = (x * lax.rsqrt(ms + RMS_EPS) * g_ref[...]).astype(BF16)

    def group(c):
        return jnp.dot(h, w_ref[:, c * GROUP_WIDTH:(c + 1) * GROUP_WIDTH], preferred_element_type=F32)

    q_a, g_a, q_b, g_b, kt_a, vt_a, kt_b, vt_b = outs[:8]
    q_a[...] = group(0).astype(BF16)
    ka = group(1)
    va_t = group(2).T
    g_a[...] = group(3)
    q_b[...] = group(4).astype(BF16)
    kb = group(5)
    vb = group(6)
    vb_t = vb.T
    g_b[...] = group(7)
    kt_a[0] = ka.T
    vt_a[0] = va_t
    kt_b[0] = kb.T
    vt_b[0] = vb_t
    if prompt:
        ka16, kb16, vta16, vtb16, kmean = outs[8:]
        ka16[...] = ka.astype(BF16)
        kb16[...] = kb.astype(BF16)
        va16 = va_t.astype(BF16)
        for c in range(vta16.shape[0]):
            vta16[c] = va16[:, c * SB_BLOCK:(c + 1) * SB_BLOCK]
        vtb16[0] = vb_t.astype(BF16)
        kmean[0] = jnp.mean(kb, axis=0, keepdims=True)
    else:
        k_b, v_b = outs[8:]
        k_b[...] = kb
        v_b[...] = vb


def _project(x2d, g_pre, w16, prompt, seq):
    m, d = x2d.shape
    n_groups = w16.shape[1] // GROUP_WIDTH
    assert n_groups == 8
    tm = MOBA_BLOCK if prompt else m
    assert m % seq == 0 and seq % tm == 0
    per_seq = seq // tm
    row = lambda i: (i, 0)
    wide = pl.BlockSpec((tm, GROUP_WIDTH), row)
    tall = pl.BlockSpec((1, GROUP_WIDTH, tm), lambda i: (i // per_seq, 0, i % per_seq))
    wide_shape = lambda dt: jax.ShapeDtypeStruct((m, GROUP_WIDTH), dt)
    tall_shape = jax.ShapeDtypeStruct((m // seq, GROUP_WIDTH, seq), F32)
    out_shape = [wide_shape(BF16), wide_shape(F32), wide_shape(BF16), wide_shape(F32)] + [tall_shape] * 4
    out_specs = [wide] * 4 + [tall] * 4
    if prompt:
        per = tm // SB_BLOCK
        out_shape += [
            wide_shape(BF16), wide_shape(BF16),
            jax.ShapeDtypeStruct((m // SB_BLOCK, GROUP_WIDTH, SB_BLOCK), BF16),
            jax.ShapeDtypeStruct((m // tm, GROUP_WIDTH, tm), BF16),
            jax.ShapeDtypeStruct((m // tm, 1, GROUP_WIDTH), F32),
        ]
        out_specs += [
            wide, wide,
            pl.BlockSpec((per, GROUP_WIDTH, SB_BLOCK), lambda i: (i, 0, 0)),
            pl.BlockSpec((1, GROUP_WIDTH, tm), lambda i: (i, 0, 0)),
            pl.BlockSpec((1, 1, GROUP_WIDTH), lambda i: (i, 0, 0)),
        ]
    else:
        out_shape += [wide_shape(F32), wide_shape(F32)]
        out_specs += [wide, wide]
    return pl.pallas_call(
        functools.partial(_proj_kernel, prompt=prompt),
        grid=(m // tm,),
        in_specs=[pl.BlockSpec((tm, d), row),
                  pl.BlockSpec((1, d), lambda i: (0, 0)),
                  pl.BlockSpec(w16.shape, lambda i: (0, 0))],
        out_specs=out_specs,
        out_shape=out_shape,
        compiler_params=pltpu.CompilerParams(dimension_semantics=("arbitrary",),
                                             vmem_limit_bytes=VMEM_LIMIT),
        name="proj_prompt" if prompt else "proj_sample",
    )(x2d, g_pre, w16)


def _sb_prompt_kernel(q_ref, k_ref, vt_ref, g_ref, y_ref, qs_ref, acc_ref, z_ref, cs_ref):
    t = SB_BLOCK
    i = pl.program_id(1)
    _store_masked_heads(q_ref[0], qs_ref, SCALE)
    key = lax.broadcasted_iota(I32, (t, t), 0)
    qry = lax.broadcasted_iota(I32, (t, t), 1)
    causal = key < qry
    upper = (qry > key).astype(BF16)
    upper2 = jnp.concatenate([upper, upper], axis=1)

    def tile(j, carry, diag):
        start = pl.multiple_of(j * t, t)
        for h in range(N_HEADS):
            pair = h // HEADS_PER_TILE
            kt = k_ref[0, pl.ds(start, t), pair * LANES:(pair + 1) * LANES]
            z_ref[h] = lax.dot_general(kt, qs_ref[h], _NT, preferred_element_type=F32)
        new_carry = []
        for h in range(N_HEADS):
            z = z_ref[h]
            sp = _softplus(z)
            lk = -sp
            if diag:
                lk = jnp.where(causal, lk, 0.0)
            z_ref[h] = z - sp
            hi, lo = _split_bf16(lk)
            cs_ref[h] = jnp.dot(upper2, jnp.concatenate([hi, lo], axis=0), preferred_element_type=F32)
            new_carry.append(carry[h] + jnp.sum(lk, axis=0, keepdims=True))
        for h in range(N_HEADS):
            a = jnp.exp(z_ref[h] + (cs_ref[h] + carry[h]))
            if diag:
                a = jnp.where(causal, a, 0.0)
            vt = vt_ref[j, h * HEAD_DIM:(h + 1) * HEAD_DIM, :]
            pv = jnp.dot(vt, a.astype(BF16), preferred_element_type=F32)
            rows = pl.ds(h * HEAD_DIM, HEAD_DIM)
            if diag:
                acc_ref[rows, :] = pv
            else:
                acc_ref[rows, :] += pv
        return tuple(new_carry)

    carry = tile(i, tuple(jnp.zeros((1, t), F32) for _ in range(N_HEADS)), True)

    def cond(s):
        j, carry = s
        top = carry[0]
        for c in carry[1:]:
            top = jnp.maximum(top, c)
        return jnp.logical_and(j >= 0, jnp.max(top) > SB_EXIT)

    def body(s):
        j, carry = s
        return j - 1, tile(j, carry, False)

    lax.while_loop(cond, body, (i - 1, carry))
    y_ref[0] = (acc_ref[...].T * _silu(g_ref[0])).astype(BF16)


def _sb_prompt(q16, k16, vt16, gate, batch, seq):
    n_blk = seq // SB_BLOCK
    tile_spec = pl.BlockSpec((1, SB_BLOCK, GROUP_WIDTH), lambda b, i: (b, i, 0))
    grid_spec = pl.GridSpec(
        grid=(batch, n_blk),
        in_specs=[tile_spec,
                  pl.BlockSpec((1, seq, GROUP_WIDTH), lambda b, i: (b, 0, 0)),
                  pl.BlockSpec((n_blk, GROUP_WIDTH, SB_BLOCK), lambda b, i: (b, 0, 0)),
                  tile_spec],
        out_specs=tile_spec,
        scratch_shapes=[pltpu.VMEM((N_HEADS, SB_BLOCK, LANES), BF16),
                        pltpu.VMEM((GROUP_WIDTH, SB_BLOCK), F32),
                        pltpu.VMEM((N_HEADS, SB_BLOCK, SB_BLOCK), F32),
                        pltpu.VMEM((N_HEADS, SB_BLOCK, SB_BLOCK), F32)],
    )
    return pl.pallas_call(
        _sb_prompt_kernel,
        grid_spec=grid_spec,
        out_shape=jax.ShapeDtypeStruct((batch, seq, GROUP_WIDTH), BF16),
        compiler_params=pltpu.CompilerParams(
            dimension_semantics=("arbitrary", "arbitrary"), vmem_limit_bytes=VMEM_LIMIT),
        name="sb_prompt",
    )(q16, k16, vt16, gate)


def _bias_tile_kernel(table_ref, o_ref):
    h = pl.program_id(0)
    t = MOBA_BLOCK
    key = lax.broadcasted_iota(I32, (t, t), 0)
    qry = lax.broadcasted_iota(I32, (t, t), 1)
    for c in range(2):
        o_ref[0, c] = _rel_bias(c * t + qry - key, lambda k: table_ref[k, h])


def _bias_tiles(rel_bias):
    n_heads = rel_bias.shape[1]
    return pl.pallas_call(
        _bias_tile_kernel,
        grid=(n_heads,),
        in_specs=[pl.BlockSpec(memory_space=pltpu.SMEM)],
        out_specs=pl.BlockSpec((1, 2, MOBA_BLOCK, MOBA_BLOCK), lambda h: (h, 0, 0, 0)),
        out_shape=jax.ShapeDtypeStruct((n_heads, 2, MOBA_BLOCK, MOBA_BLOCK), F32),
        name="moba_bias_tiles",
    )(rel_bias)


def _gate_schedule(batch, n_blk, total_blocks):
    n_own = batch * n_blk
    n_prev = batch * (n_blk - 1)
    n_far = batch * (n_blk - 1) * (n_blk - 2) // 2
    for far in range(N_HEADS, -1, -1):
        for prev in range(N_HEADS, -1, -1):
            rest = total_blocks - far * n_far - prev * n_prev
            if rest >= 0 and rest % n_own == 0 and rest // n_own <= N_HEADS:
                return rest // n_own, prev, far
    raise ValueError("no gating schedule for these shapes")


def _moba_prompt_kernel(pt_ref, table_ref, q_ref, k_ref, vt_ref, kmean_ref, bias_ref, g_ref, sq_ref,
                        kpool_ref, y_ref, top_ref, qs_ref, mask_ref, m_ref, acc_ref, s_ref,
                        buf, sem, ksum, cnt_ref, *, n_pages, schedule):
    t = MOBA_BLOCK
    i = pl.program_id(1)
    n_blk = kmean_ref.shape[1]
    n_seq = top_ref.shape[0]
    n_slots, _, page = buf.shape
    pages_per_block = MOBA_BLOCK // page
    blocks_per_seq = n_pages // pages_per_block
    last_page = n_seq * n_pages - 1
    gate_own, gate_prev, gate_far = schedule

    def page_copy(g):
        slot = g % n_slots
        return pltpu.make_async_copy(kpool_ref.at[pt_ref[jnp.minimum(g, last_page)]], buf.at[slot], sem.at[slot])

    @pl.when(jnp.logical_and(pl.program_id(0) == 0, i == 0))
    def _():
        cnt_ref[0] = 0
        ksum[...] = jnp.zeros(ksum.shape, F32)
        for g in range(n_slots):
            page_copy(g).start()

    col = lax.broadcasted_iota(I32, (N_HEADS, LANES), 1)

    def gate_block(b):
        qbd, _ = _block_diag_rows(sq_ref[b // blocks_per_seq])
        scores = None
        for c in range(pages_per_block):
            g = b * pages_per_block + c
            z = jnp.dot(qbd, buf[g % n_slots].astype(BF16), preferred_element_type=F32)
            scores = z if scores is None else scores + z
        return jnp.sum(scores, axis=1, keepdims=True) * (1.0 / MOBA_BLOCK)

    def finish_sequence(seq):
        gate = ksum[...]
        gate = jnp.where(seq % 2 == 1, pltpu.roll(gate, LANES - blocks_per_seq, 1), gate)
        blk = lax.broadcasted_iota(I32, gate.shape, 1)
        rank = jnp.zeros(gate.shape, I32)
        for m in range(blocks_per_seq):
            gm = gate[:, m:m + 1]
            beats = jnp.logical_or(gm > gate, jnp.logical_and(gm == gate, m < blk))
            rank = rank + jnp.where(beats, 1, 0)
        top = jnp.zeros((N_HEADS, LANES), I32)
        for j in range(MOBA_TOPK):
            hit = jnp.logical_and(rank == j, blk < blocks_per_seq)
            idx = jnp.sum(jnp.where(hit, blk, 0), axis=1, keepdims=True)
            top = jnp.where(blk == j, idx, top)
        top_ref[seq] = top

    q = q_ref[0]
    _store_masked_heads(q, qs_ref, SCALE)
    key = lax.broadcasted_iota(I32, (t, t), 0)
    qry = lax.broadcasted_iota(I32, (t, t), 1)
    blk = lax.broadcasted_iota(I32, (n_blk, t), 0)
    lane = lax.broadcasted_iota(I32, (n_blk, LANES), 1)

    for h in range(N_HEADS):
        pair, hh = divmod(h, HEADS_PER_TILE)
        km = kmean_ref[0, :, pair * LANES:(pair + 1) * LANES]
        km = jnp.where(lane // HEAD_DIM == hh, km, 0.0).astype(BF16)
        gate = lax.dot_general(km, q[:, pair * LANES:(pair + 1) * LANES], _NT,
                               preferred_element_type=F32)
        rank = jnp.zeros((n_blk, t), I32)
        for m in range(n_blk):
            gm = gate[m:m + 1, :]
            beats = jnp.logical_or(gm > gate, jnp.logical_and(gm == gate, m < blk))
            rank = rank + jnp.where(beats, (m < i).astype(I32), 0)
        sel = jnp.logical_and(rank < MOBA_TOPK, blk < i)
        mask_ref[h] = jnp.where(sel, 0.0, -jnp.inf)

    ones = jnp.ones((ONES_ROWS, t), BF16)

    def tile(n, bias, shift, first, n_gate):
        start = pl.multiple_of(n * t, t)
        alphas, offsets = [], []
        for h in range(N_HEADS):
            pair = h // HEADS_PER_TILE
            row = pl.ds(h, 1)
            kt = k_ref[0, pl.ds(start, t), pair * LANES:(pair + 1) * LANES]
            s = lax.dot_general(kt, qs_ref[h], _NT, preferred_element_type=F32)
            if bias is not None:
                s = s + bias(h)
            s_ref[h] = s
            m_tile = jnp.max(s, axis=0, keepdims=True)
            if shift is not None:
                m_tile = m_tile + shift(h)
            if first:
                m_new = m_tile
            else:
                m_old = m_ref[row, :]
                m_new = jnp.maximum(m_old, m_tile)
                alphas.append(jnp.exp(m_old - m_new))
            m_ref[row, :] = m_new
            offsets.append(m_new if shift is None else m_new - shift(h))
        done = cnt_ref[0]
        pages = [done * pages_per_block + u for u in range(n_gate * pages_per_block)]
        for g in pages:
            page_copy(g).wait()
        values = []
        for h in range(N_HEADS):
            p = jnp.exp(s_ref[h] - offsets[h]).astype(BF16)
            vt = jnp.concatenate([vt_ref[n, pl.ds(h * HEAD_DIM, HEAD_DIM), :], ones], axis=0)
            pv = jnp.dot(vt, p, preferred_element_type=F32)
            acc_ref[h] = pv if first else alphas[h] * acc_ref[h] + pv
            if h < n_gate:
                values.append(gate_block(done + h))
        for g in pages:
            page_copy(g + n_slots).start()
        if n_gate:
            gates = ksum[...]
            for u, value in enumerate(values):
                gates = jnp.where(col == (done + u) % (2 * blocks_per_seq), value, gates)
            ksum[...] = gates
            cnt_ref[0] = done + n_gate

            @pl.when((done + n_gate) // blocks_per_seq > done // blocks_per_seq)
            def _():
                finish_sequence((done + n_gate) // blocks_per_seq - 1)

    tile(i, lambda h: jnp.where(key <= qry, bias_ref[h, 0], -jnp.inf), None, True, gate_own)

    @pl.when(i >= 1)
    def _():
        n = i - 1
        tile(n, lambda h: bias_ref[h, 1], lambda h: mask_ref[h, pl.ds(n, 1), :], False, gate_prev)

    def far(n, _):
        tile(n, None, lambda h: mask_ref[h, pl.ds(n, 1), :] + table_ref[REL_BUCKETS - 1, h], False, gate_far)
        return 0

    lax.fori_loop(0, jnp.maximum(i - 1, 0), far, 0)

    outs = []
    for h in range(N_HEADS):
        acc = acc_ref[h]
        outs.append(acc[:HEAD_DIM] / acc[HEAD_DIM:HEAD_DIM + 1])
    y_ref[0] = (jnp.concatenate(outs, axis=0).T * _silu(g_ref[0])).astype(BF16)

    @pl.when(jnp.logical_and(pl.program_id(0) == pl.num_programs(0) - 1, i == n_blk - 1))
    def _():
        for g in range(n_slots):
            page_copy(g).wait()


def _moba_prompt(pt_flat, rel_bias, q16, k16, vt16, kmean, bias_tiles, gate, sample_q16, kpool,
                 batch, seq, n_pages):
    n_blk = seq // MOBA_BLOCK
    assert n_blk > MOBA_TOPK
    n_seq = sample_q16.shape[0]
    page = kpool.shape[2]
    assert MOBA_BLOCK % page == 0 and n_pages % (MOBA_BLOCK // page) == 0
    blocks_per_seq = n_pages // (MOBA_BLOCK // page)
    assert MOBA_TOPK <= blocks_per_seq and 2 * blocks_per_seq <= LANES
    schedule = _gate_schedule(batch, n_blk, n_seq * blocks_per_seq)
    n_slots = min(32, n_seq * n_pages)
    tile_spec = pl.BlockSpec((1, MOBA_BLOCK, GROUP_WIDTH), lambda b, i, pt: (b, i, 0))
    whole = lambda shape: pl.BlockSpec(shape, lambda b, i, pt: (0,) * len(shape))
    grid_spec = pltpu.PrefetchScalarGridSpec(
        num_scalar_prefetch=1,
        grid=(batch, n_blk),
        in_specs=[pl.BlockSpec(memory_space=pltpu.SMEM),
                  tile_spec,
                  pl.BlockSpec((1, seq, GROUP_WIDTH), lambda b, i, pt: (b, 0, 0)),
                  pl.BlockSpec((n_blk, GROUP_WIDTH, MOBA_BLOCK), lambda b, i, pt: (b, 0, 0)),
                  pl.BlockSpec((1, n_blk, GROUP_WIDTH), lambda b, i, pt: (b, 0, 0)),
                  whole((N_HEADS, 2, MOBA_BLOCK, MOBA_BLOCK)),
                  tile_spec,
                  whole((n_seq, 1, GROUP_WIDTH)),
                  pl.BlockSpec(memory_space=pl.ANY)],
        out_specs=[tile_spec, whole((n_seq, N_HEADS, LANES))],
        scratch_shapes=[pltpu.VMEM((N_HEADS, MOBA_BLOCK, LANES), BF16),
                        pltpu.VMEM((N_HEADS, n_blk, MOBA_BLOCK), F32),
                        pltpu.VMEM((N_HEADS, MOBA_BLOCK), F32),
                        pltpu.VMEM((N_HEADS, HEAD_DIM + ONES_ROWS, MOBA_BLOCK), F32),
                        pltpu.VMEM((N_HEADS, MOBA_BLOCK, MOBA_BLOCK), F32),
                        pltpu.VMEM((n_slots, GROUP_WIDTH, page), F32),
                        pltpu.SemaphoreType.DMA((n_slots,)),
                        pltpu.VMEM((N_HEADS, LANES), F32),
                        pltpu.SMEM((1,), I32)],
    )
    return pl.pallas_call(
        functools.partial(_moba_prompt_kernel, n_pages=n_pages, schedule=schedule),
        grid_spec=grid_spec,
        out_shape=[jax.ShapeDtypeStruct((batch, seq, GROUP_WIDTH), BF16),
                   jax.ShapeDtypeStruct((n_seq, N_HEADS, LANES), I32)],
        compiler_params=pltpu.CompilerParams(
            dimension_semantics=("arbitrary", "arbitrary"), vmem_limit_bytes=VMEM_LIMIT),
        name="moba_prompt",
    )(pt_flat, rel_bias, q16, k16, vt16, kmean, bias_tiles, gate, sample_q16, kpool)


def _mix_kernel(x_ref, a_ref, b_ref, *rest, gated):
    if gated:
        w_ref, g_ref, y_ref = rest
        ya, yb = a_ref[...], b_ref[...]
    else:
        ga_ref, gb_ref, w_ref, g_ref, y_ref = rest
        ya = (a_ref[...] * _silu(ga_ref[...])).astype(BF16)
        yb = (b_ref[...] * _silu(gb_ref[...])).astype(BF16)
    y = (jnp.dot(ya, w_ref[:GROUP_WIDTH, :], preferred_element_type=F32)
         + jnp.dot(yb, w_ref[GROUP_WIDTH:, :], preferred_element_type=F32))
    ms = jnp.mean(y * y, axis=-1, keepdims=True)
    y_ref[...] = x_ref[...] + y * lax.rsqrt(ms + RMS_EPS) * g_ref[...]


def _mix_out(x2d, a, b, gates, w16, g_post):
    m, d = x2d.shape
    tm = min(m, 256)
    assert m % tm == 0
    row = lambda i: (i, 0)
    wide = pl.BlockSpec((tm, GROUP_WIDTH), row)
    extra = () if gates is None else tuple(gates)
    return pl.pallas_call(
        functools.partial(_mix_kernel, gated=gates is None),
        grid=(m // tm,),
        in_specs=[pl.BlockSpec((tm, d), row), wide, wide] + [wide] * len(extra)
                 + [pl.BlockSpec(w16.shape, lambda i: (0, 0)), pl.BlockSpec((1, d), lambda i: (0, 0))],
        out_specs=pl.BlockSpec((tm, d), row),
        out_shape=jax.ShapeDtypeStruct((m, d), F32),
        compiler_params=pltpu.CompilerParams(dimension_semantics=("arbitrary",),
                                             vmem_limit_bytes=VMEM_LIMIT),
        name="mix_out",
    )(x2d, a, b, *extra, w16, g_post)


def _block_diag_rows(row):
    shape = (N_HEADS, GROUP_WIDTH)
    own = lax.broadcasted_iota(I32, shape, 1) // HEAD_DIM == lax.broadcasted_iota(I32, shape, 0)
    return jnp.where(own, jnp.broadcast_to(row.astype(F32), shape), 0.0).astype(row.dtype), own


def _sb_sample_kernel(pt_ref, q_ref, k1_ref, k2_ref, v1_ref, v2_ref, kpool_ref, vpool_ref, o_ref,
                      kbuf, vbuf, sem, *, n_pages):
    b = pl.program_id(0)
    page = kbuf.shape[1]
    qbd, own = _block_diag_rows(q_ref[0] * SCALE)
    lower = (lax.broadcasted_iota(I32, (page, page), 0)
             > lax.broadcasted_iota(I32, (page, page), 1)).astype(BF16)
    lower2 = jnp.concatenate([lower, lower], axis=0)

    def step(kp, vp, carry, acc):
        z = jnp.dot(qbd, kp.astype(BF16), preferred_element_type=F32)
        sp = _softplus(z)
        lk = -sp
        hi, lo = _split_bf16(lk)
        after = carry + jnp.dot(jnp.concatenate([hi, lo], axis=1), lower2, preferred_element_type=F32)
        a = jnp.exp((z - sp) + after)
        acc = acc + lax.dot_general(a.astype(BF16), vp.astype(BF16), _NT,
                                    preferred_element_type=F32)
        return carry + jnp.sum(lk, axis=1, keepdims=True), acc

    carry = jnp.zeros((N_HEADS, 1), F32)
    acc = jnp.zeros((N_HEADS, GROUP_WIDTH), F32)
    carry, acc = step(k1_ref[0], v1_ref[0], carry, acc)
    carry, acc = step(k2_ref[0], v2_ref[0], carry, acc)

    def cond(s):
        p, carry, _ = s
        return jnp.logical_and(p >= 0, jnp.max(carry) > SB_EXIT)

    def body(s):
        p, carry, acc = s
        phys = pt_ref[b * n_pages + p]
        ck = pltpu.make_async_copy(kpool_ref.at[phys], kbuf, sem.at[0])
        cv = pltpu.make_async_copy(vpool_ref.at[phys], vbuf, sem.at[1])
        ck.start()
        cv.start()
        ck.wait()
        cv.wait()
        carry, acc = step(kbuf[...], vbuf[...], carry, acc)
        return p - 1, carry, acc

    _, _, acc = lax.while_loop(cond, body, (n_pages - 3, carry, acc))
    o_ref[0] = jnp.sum(jnp.where(own, acc, 0.0), axis=0, keepdims=True)


def _sb_sample(pt_flat, q16, kpool, vpool, n_seq, n_pages):
    page = kpool.shape[2]
    assert n_pages >= 2
    last = lambda off: (lambda b, pt: (pt[b * n_pages + n_pages - off], 0, 0))
    page_spec = lambda off: pl.BlockSpec((1, GROUP_WIDTH, page), last(off))
    grid_spec = pltpu.PrefetchScalarGridSpec(
        num_scalar_prefetch=1,
        grid=(n_seq,),
        in_specs=[pl.BlockSpec((1, 1, GROUP_WIDTH), lambda b, pt: (b, 0, 0)),
                  page_spec(1), page_spec(2), page_spec(1), page_spec(2),
                  pl.BlockSpec(memory_space=pl.ANY), pl.BlockSpec(memory_space=pl.ANY)],
        out_specs=pl.BlockSpec((1, 1, GROUP_WIDTH), lambda b, pt: (b, 0, 0)),
        scratch_shapes=[pltpu.VMEM((GROUP_WIDTH, page), F32), pltpu.VMEM((GROUP_WIDTH, page), F32),
                        pltpu.SemaphoreType.DMA((2,))],
    )
    return pl.pallas_call(
        functools.partial(_sb_sample_kernel, n_pages=n_pages),
        grid_spec=grid_spec,
        out_shape=jax.ShapeDtypeStruct((n_seq, 1, GROUP_WIDTH), F32),
        compiler_params=pltpu.CompilerParams(dimension_semantics=("arbitrary",),
                                             vmem_limit_bytes=VMEM_LIMIT),
        name="sb_sample",
    )(pt_flat, q16, kpool, kpool, vpool, vpool, kpool, vpool)


def _moba_sample_kernel(pt_ref, top_ref, topv_ref, table_ref, q_ref, knew_ref, vnew_ref,
                        kpool_ref, vpool_ref, o_ref, kbuf, vbuf, sem, *, n_pages):
    b = pl.program_id(0)
    n_seq = pl.num_programs(0)
    page = kpool_ref.shape[3]
    pages_per_block = MOBA_BLOCK // page
    n_sel = MOBA_TOPK * MOBA_BLOCK
    past_len = n_pages * page

    def copies(seq, slot):
        out = []
        for h in range(N_HEADS):
            for j in range(MOBA_TOPK):
                blk = top_ref[(seq * N_HEADS + h) * MOBA_TOPK + j]
                for c in range(pages_per_block):
                    phys = pt_ref[seq * n_pages + blk * pages_per_block + c]
                    cols = pl.ds((j * pages_per_block + c) * page, page)
                    out.append(pltpu.make_async_copy(kpool_ref.at[phys, h],
                                                     kbuf.at[slot, h, :, cols], sem.at[slot, 0]))
                    out.append(pltpu.make_async_copy(vpool_ref.at[phys, h],
                                                     vbuf.at[slot, h, :, cols], sem.at[slot, 1]))
        return out

    slot = b % 2

    @pl.when(b == 0)
    def _():
        for c in copies(0, 0):
            c.start()

    @pl.when(b + 1 < n_seq)
    def _():
        for c in copies(b + 1, 1 - slot):
            c.start()

    for c in copies(b, slot):
        c.wait()

    q_f = q_ref[0].astype(F32)
    head = lax.broadcasted_iota(I32, (N_HEADS, n_sel), 0)
    s = jnp.zeros((N_HEADS, n_sel), F32)
    for h in range(N_HEADS):
        q8 = jnp.broadcast_to(q_f[h:h + 1, :] * SCALE, (8, HEAD_DIM)).astype(BF16)
        s_h = jnp.dot(q8, kbuf[slot, h].astype(BF16), preferred_element_type=F32)
        s = jnp.where(head == h, s_h, s)
    pos = lax.broadcasted_iota(I32, (N_HEADS, n_sel), 1)
    which = pos // MOBA_BLOCK
    topv = topv_ref[0]
    blk = jnp.zeros((N_HEADS, n_sel), I32)
    for j in range(MOBA_TOPK):
        blk = jnp.where(which == j, topv[:, j:j + 1], blk)
    rel = past_len - (blk * MOBA_BLOCK + pos % MOBA_BLOCK)
    s = s + _rel_bias(rel, lambda k: table_ref[:, k:k + 1])
    knew = knew_ref[0].astype(BF16).astype(F32)
    own = jnp.sum(q_f * knew, axis=1, keepdims=True) * SCALE + table_ref[:, 0:1]
    m = jnp.maximum(jnp.max(s, axis=1, keepdims=True), own)
    p = jnp.exp(s - m)
    p_own = jnp.exp(own - m)
    denom = jnp.sum(p, axis=1, keepdims=True) + p_own
    p16 = p.astype(BF16)
    head_o = lax.broadcasted_iota(I32, (N_HEADS, HEAD_DIM), 0)
    pv = jnp.zeros((N_HEADS, HEAD_DIM), F32)
    for h in range(N_HEADS):
        pv_h = lax.dot_general(p16, vbuf[slot, h].astype(BF16), _NT, preferred_element_type=F32)
        pv = jnp.where(head_o == h, pv_h, pv)
    vnew = vnew_ref[0].astype(BF16).astype(F32)
    o_ref[0] = (pv + p_own.astype(BF16).astype(F32) * vnew) / denom


def _moba_sample(pt_flat, top, rel_bias, q16, k_new, v_new, kpool, vpool, n_seq, n_pages):
    n_sel = MOBA_TOPK * MOBA_BLOCK
    head_rows = pl.BlockSpec((1, N_HEADS, HEAD_DIM), lambda b, pt, tp: (b, 0, 0))
    whole = lambda a: pl.BlockSpec(a.shape, lambda b, pt, tp: (0,) * a.ndim)
    table_t = rel_bias.T
    grid_spec = pltpu.PrefetchScalarGridSpec(
        num_scalar_prefetch=2,
        grid=(n_seq,),
        in_specs=[pl.BlockSpec((1, N_HEADS, LANES), lambda b, pt, tp: (b, 0, 0)),
                  whole(table_t),
                  head_rows, head_rows, head_rows,
                  pl.BlockSpec(memory_space=pl.ANY), pl.BlockSpec(memory_space=pl.ANY)],
        out_specs=head_rows,
        scratch_shapes=[pltpu.VMEM((2, N_HEADS, HEAD_DIM, n_sel), F32),
                        pltpu.VMEM((2, N_HEADS, HEAD_DIM, n_sel), F32),
                        pltpu.SemaphoreType.DMA((2, 2))],
    )
    return pl.pallas_call(
        functools.partial(_moba_sample_kernel, n_pages=n_pages),
        grid_spec=grid_spec,
        out_shape=jax.ShapeDtypeStruct((n_seq, N_HEADS, HEAD_DIM), F32),
        compiler_params=pltpu.CompilerParams(dimension_semantics=("arbitrary",),
                                             vmem_limit_bytes=VMEM_LIMIT),
        name="moba_sample",
    )(pt_flat, top[:, :, :MOBA_TOPK].reshape(-1), top, table_t, q16, k_new, v_new, kpool, vpool)


def kernel(x_prompt, x_sample, cache_k_sb, cache_v_sb, cache_k_moba, cache_v_moba, page_table,
           w_in, w_out, g_pre, g_post, rel_bias):
    depth, d_model, d_in = w_in.shape
    assert depth == 1 and d_in == 8 * GROUP_WIDTH and w_out.shape[1] == 2 * GROUP_WIDTH
    batch, seq, _ = x_prompt.shape
    n_seq, dec_seq, _ = x_sample.shape
    assert dec_seq == 1 and seq % MOBA_BLOCK == 0
    n_pool, page = cache_k_sb.shape[1:3]
    n_pages = page_table.shape[1]

    w_in16 = w_in[0].astype(BF16)
    w_out16 = w_out[0].astype(BF16)
    pt_flat = page_table.reshape(-1).astype(I32)
    pool4 = lambda c: jnp.transpose(c[0], (0, 2, 3, 1))
    pool3 = lambda c: pool4(c).reshape(n_pool, GROUP_WIDTH, page)

    (q_a, g_a, q_b, g_b, kt_a, vt_a, kt_b, vt_b, ka16, kb16, vta16, vtb16, kmean_b) = _project(
        x_prompt.reshape(batch * seq, d_model), g_pre, w_in16, True, seq)
    sq_a, sg_a, sq_b, sg_b, skt_a, svt_a, skt_b, svt_b, sk_b, sv_b = _project(
        x_sample.reshape(n_seq, d_model), g_pre, w_in16, False, n_seq)

    as_seq = lambda a: a.reshape(batch, seq, GROUP_WIDTH)
    y_a = _sb_prompt(as_seq(q_a), as_seq(ka16), vta16, as_seq(g_a), batch, seq)
    y_b, top = _moba_prompt(pt_flat, rel_bias, as_seq(q_b), as_seq(kb16), vtb16,
                            kmean_b.reshape(batch, seq // MOBA_BLOCK, GROUP_WIDTH),
                            _bias_tiles(rel_bias), as_seq(g_b),
                            sq_b.reshape(n_seq, 1, GROUP_WIDTH), pool3(cache_k_moba), batch, seq, n_pages)
    y_prompt = _mix_out(x_prompt.reshape(batch * seq, d_model), y_a.reshape(batch * seq, GROUP_WIDTH),
                        y_b.reshape(batch * seq, GROUP_WIDTH), None, w_out16, g_post)

    so_a = _sb_sample(pt_flat, sq_a.reshape(n_seq, 1, GROUP_WIDTH), pool3(cache_k_sb), pool3(cache_v_sb),
                      n_seq, n_pages)
    heads = lambda a: a.reshape(n_seq, N_HEADS, HEAD_DIM)
    so_b = _moba_sample(pt_flat, top, rel_bias, heads(sq_b), heads(sk_b), heads(sv_b),
                        pool4(cache_k_moba), pool4(cache_v_moba), n_seq, n_pages)
    y_sample = _mix_out(x_sample.reshape(n_seq, d_model), so_a.reshape(n_seq, GROUP_WIDTH),
                        so_b.reshape(n_seq, GROUP_WIDTH), (sg_a, sg_b), w_out16, g_post)

    kv = lambda a: jnp.transpose(a.reshape(a.shape[0], N_HEADS, HEAD_DIM, a.shape[2]), (0, 3, 1, 2))[None]
    kv_s = lambda a: jnp.transpose(kv(a), (0, 2, 1, 3, 4))
    return (y_prompt.reshape(batch, seq, d_model), y_sample.reshape(n_seq, 1, d_model),
            kv(kt_a), kv(vt_a), kv(kt_b), kv(vt_b),
            kv_s(skt_a), kv_s(svt_a), kv_s(skt_b), kv_s(svt_b))
```

```python
import functools
import math

import jax
import jax.numpy as jnp
from jax import lax
from jax.experimental import pallas as pl
from jax.experimental.pallas import tpu as pltpu

F32 = jnp.float32
BF16 = jnp.bfloat16
I32 = jnp.int32

HEAD_DIM = 64
LANES = 128
HEADS_PER_TILE = LANES // HEAD_DIM
GROUP_WIDTH = 512
N_HEADS = GROUP_WIDTH // HEAD_DIM
N_PAIRS = N_HEADS // HEADS_PER_TILE
SB_BLOCK = 128
MOBA_BLOCK = 256
MOBA_TOPK = 3
ONES_ROWS = 16
SAMPLE_BUFFERS = 3
REL_BUCKETS = 32
REL_MAX_DIST = 128
RMS_EPS = 1e-6
SCALE = HEAD_DIM ** -0.5
SB_EXIT = -110.0
VMEM_LIMIT = 56 * 1024 * 1024

_NT = (((1,), (1,)), ((), ()))


def _softplus(z):
    return jnp.maximum(z, 0.0) + jnp.log(1.0 + jnp.exp(-jnp.abs(z)))


def _split_bf16(x):
    hi = x.astype(BF16)
    lo = (x - hi.astype(F32)).astype(BF16)
    return hi, lo


def _rel_bias(rel, table):
    max_exact = REL_BUCKETS // 2
    n = jnp.maximum(rel, 0)
    nf = jnp.maximum(n, 1).astype(F32)
    large = max_exact + (jnp.log(nf / max_exact) / math.log(REL_MAX_DIST / max_exact)
                         * (REL_BUCKETS - max_exact)).astype(I32)
    large = jnp.minimum(large, REL_BUCKETS - 1)
    bucket = jnp.where(n < max_exact, n, large)
    out = jnp.broadcast_to(table(REL_BUCKETS - 1), rel.shape).astype(F32)
    for k in range(REL_BUCKETS - 1):
        out = jnp.where(bucket == k, table(k), out)
    return out


def _silu(g):
    return g * jax.nn.sigmoid(g)


def _store_masked_heads(q, qs_ref, scale):
    rows = q.shape[0]
    lane = lax.broadcasted_iota(I32, (rows, LANES), 1)
    for pair in range(N_PAIRS):
        q32 = q[:, pair * LANES:(pair + 1) * LANES].astype(F32) * scale
        for hh in range(HEADS_PER_TILE):
            qs_ref[pair * HEADS_PER_TILE + hh] = jnp.where(lane // HEAD_DIM == hh, q32, 0.0).astype(BF16)


def _proj_kernel(x_ref, g_ref, w_ref, *outs, prompt):
    x = x_ref[...]
    ms = jnp.mean(x * x, axis=-1, keepdims=True)
    h = (x * lax.rsqrt(ms + RMS_EPS) * g_ref[...]).astype(BF16)

    def group(c):
        return jnp.dot(h, w_ref[:, c * GROUP_WIDTH:(c + 1) * GROUP_WIDTH], preferred_element_type=F32)

    q_a, g_a, q_b, g_b, kt_a, vt_a, kt_b, vt_b = outs[:8]
    q_a[...] = group(0).astype(BF16)
    ka = group(1)
    va_t = group(2).T
    g_a[...] = group(3)
    q_b[...] = group(4).astype(BF16)
    kb = group(5)
    vb = group(6)
    vb_t = vb.T
    g_b[...] = group(7)
    kt_a[0] = ka.T
    vt_a[0] = va_t
    kt_b[0] = kb.T
    vt_b[0] = vb_t
    if prompt:
        ka16, kb16, vta16, vtb16, kmean = outs[8:]
        ka16[...] = ka.astype(BF16)
        kb16[...] = kb.astype(BF16)
        va16 = va_t.astype(BF16)
        for c in range(vta16.shape[0]):
            vta16[c] = va16[:, c * SB_BLOCK:(c + 1) * SB_BLOCK]
        vtb16[0] = vb_t.astype(BF16)
        kmean[0] = jnp.mean(kb, axis=0, keepdims=True)
    else:
        k_b, v_b = outs[8:]
        k_b[...] = kb
        v_b[...] = vb


def _project(x2d, g_pre, w16, prompt, seq):
    m, d = x2d.shape
    n_groups = w16.shape[1] // GROUP_WIDTH
    assert n_groups == 8
    tm = MOBA_BLOCK if prompt else m
    assert m % seq == 0 and seq % tm == 0
    per_seq = seq // tm
    row = lambda i: (i, 0)
    wide = pl.BlockSpec((tm, GROUP_WIDTH), row)
    tall = pl.BlockSpec((1, GROUP_WIDTH, tm), lambda i: (i // per_seq, 0, i % per_seq))
    wide_shape = lambda dt: jax.ShapeDtypeStruct((m, GROUP_WIDTH), dt)
    tall_shape = jax.ShapeDtypeStruct((m // seq, GROUP_WIDTH, seq), F32)
    out_shape = [wide_shape(BF16), wide_shape(F32), wide_shape(BF16), wide_shape(F32)] + [tall_shape] * 4
    out_specs = [wide] * 4 + [tall] * 4
    if prompt:
        per = tm // SB_BLOCK
        out_shape += [
            wide_shape(BF16), wide_shape(BF16),
            jax.ShapeDtypeStruct((m // SB_BLOCK, GROUP_WIDTH, SB_BLOCK), BF16),
            jax.ShapeDtypeStruct((m // tm, GROUP_WIDTH, tm), BF16),
            jax.ShapeDtypeStruct((m // tm, 1, GROUP_WIDTH), F32),
        ]
        out_specs += [
            wide, wide,
            pl.BlockSpec((per, GROUP_WIDTH, SB_BLOCK), lambda i: (i, 0, 0)),
            pl.BlockSpec((1, GROUP_WIDTH, tm), lambda i: (i, 0, 0)),
            pl.BlockSpec((1, 1, GROUP_WIDTH), lambda i: (i, 0, 0)),
        ]
    else:
        out_shape += [wide_shape(F32), wide_shape(F32)]
        out_specs += [wide, wide]
    return pl.pallas_call(
        functools.partial(_proj_kernel, prompt=prompt),
        grid=(m // tm,),
        in_specs=[pl.BlockSpec((tm, d), row),
                  pl.BlockSpec((1, d), lambda i: (0, 0)),
                  pl.BlockSpec(w16.shape, lambda i: (0, 0))],
        out_specs=out_specs,
        out_shape=out_shape,
        compiler_params=pltpu.CompilerParams(dimension_semantics=("arbitrary",),
                                             vmem_limit_bytes=VMEM_LIMIT),
        name="proj_prompt" if prompt else "proj_sample",
    )(x2d, g_pre, w16)


def _sb_prompt_kernel(q_ref, k_ref, vt_ref, g_ref, y_ref, qs_ref, acc_ref, z_ref, cs_ref):
    t = SB_BLOCK
    i = pl.program_id(1)
    _store_masked_heads(q_ref[0], qs_ref, SCALE)
    key = lax.broadcasted_iota(I32, (t, t), 0)
    qry = lax.broadcasted_iota(I32, (t, t), 1)
    causal = key < qry
    upper = (qry > key).astype(BF16)
    upper2 = jnp.concatenate([upper, upper], axis=1)

    def tile(j, carry, diag):
        start = pl.multiple_of(j * t, t)
        for h in range(N_HEADS):
            pair = h // HEADS_PER_TILE
            kt = k_ref[0, pl.ds(start, t), pair * LANES:(pair + 1) * LANES]
            z_ref[h] = lax.dot_general(kt, qs_ref[h], _NT, preferred_element_type=F32)
        new_carry = []
        for h in range(N_HEADS):
            z = z_ref[h]
            sp = _softplus(z)
            lk = -sp
            if diag:
                lk = jnp.where(causal, lk, 0.0)
            z_ref[h] = z - sp
            hi, lo = _split_bf16(lk)
            cs_ref[h] = jnp.dot(upper2, jnp.concatenate([hi, lo], axis=0), preferred_element_type=F32)
            new_carry.append(carry[h] + jnp.sum(lk, axis=0, keepdims=True))
        for h in range(N_HEADS):
            a = jnp.exp(z_ref[h] + (cs_ref[h] + carry[h]))
            if diag:
                a = jnp.where(causal, a, 0.0)
            vt = vt_ref[j, h * HEAD_DIM:(h + 1) * HEAD_DIM, :]
            pv = jnp.dot(vt, a.astype(BF16), preferred_element_type=F32)
            rows = pl.ds(h * HEAD_DIM, HEAD_DIM)
            if diag:
                acc_ref[rows, :] = pv
            else:
                acc_ref[rows, :] += pv
        return tuple(new_carry)

    carry = tile(i, tuple(jnp.zeros((1, t), F32) for _ in range(N_HEADS)), True)

    def cond(s):
        j, carry = s
        top = carry[0]
        for c in carry[1:]:
            top = jnp.maximum(top, c)
        return jnp.logical_and(j >= 0, jnp.max(top) > SB_EXIT)

    def body(s):
        j, carry = s
        return j - 1, tile(j, carry, False)

    lax.while_loop(cond, body, (i - 1, carry))
    y_ref[0] = (acc_ref[...].T * _silu(g_ref[0])).astype(BF16)


def _sb_prompt(q16, k16, vt16, gate, batch, seq):
    n_blk = seq // SB_BLOCK
    tile_spec = pl.BlockSpec((1, SB_BLOCK, GROUP_WIDTH), lambda b, i: (b, i, 0))
    grid_spec = pl.GridSpec(
        grid=(batch, n_blk),
        in_specs=[tile_spec,
                  pl.BlockSpec((1, seq, GROUP_WIDTH), lambda b, i: (b, 0, 0)),
                  pl.BlockSpec((n_blk, GROUP_WIDTH, SB_BLOCK), lambda b, i: (b, 0, 0)),
                  tile_spec],
        out_specs=tile_spec,
        scratch_shapes=[pltpu.VMEM((N_HEADS, SB_BLOCK, LANES), BF16),
                        pltpu.VMEM((GROUP_WIDTH, SB_BLOCK), F32),
                        pltpu.VMEM((N_HEADS, SB_BLOCK, SB_BLOCK), F32),
                        pltpu.VMEM((N_HEADS, SB_BLOCK, SB_BLOCK), F32)],
    )
    return pl.pallas_call(
        _sb_prompt_kernel,
        grid_spec=grid_spec,
        out_shape=jax.ShapeDtypeStruct((batch, seq, GROUP_WIDTH), BF16),
        compiler_params=pltpu.CompilerParams(
            dimension_semantics=("arbitrary", "arbitrary"), vmem_limit_bytes=VMEM_LIMIT),
        name="sb_prompt",
    )(q16, k16, vt16, gate)


def _bias_tile_kernel(table_ref, o_ref):
    h = pl.program_id(0)
    t = MOBA_BLOCK
    key = lax.broadcasted_iota(I32, (t, t), 0)
    qry = lax.broadcasted_iota(I32, (t, t), 1)
    for c in range(2):
        o_ref[0, c] = _rel_bias(c * t + qry - key, lambda k: table_ref[k, h])


def _bias_tiles(rel_bias):
    n_heads = rel_bias.shape[1]
    return pl.pallas_call(
        _bias_tile_kernel,
        grid=(n_heads,),
        in_specs=[pl.BlockSpec(memory_space=pltpu.SMEM)],
        out_specs=pl.BlockSpec((1, 2, MOBA_BLOCK, MOBA_BLOCK), lambda h: (h, 0, 0, 0)),
        out_shape=jax.ShapeDtypeStruct((n_heads, 2, MOBA_BLOCK, MOBA_BLOCK), F32),
        name="moba_bias_tiles",
    )(rel_bias)


def _gate_schedule(batch, n_blk, total_blocks):
    n_own = batch * n_blk
    n_prev = batch * (n_blk - 1)
    n_far = batch * (n_blk - 1) * (n_blk - 2) // 2
    for far in range(N_HEADS, -1, -1):
        for prev in range(N_HEADS, -1, -1):
            rest = total_blocks - far * n_far - prev * n_prev
            if rest >= 0 and rest % n_own == 0 and rest // n_own <= N_HEADS:
                return rest // n_own, prev, far
    raise ValueError("no gating schedule for these shapes")


def _moba_prompt_kernel(pt_ref, table_ref, q_ref, k_ref, vt_ref, kmean_ref, bias_ref, g_ref, sq_ref,
                        kpool_ref, y_ref, top_ref, qs_ref, mask_ref, m_ref, acc_ref, s_ref,
                        buf, sem, ksum, cnt_ref, *, n_pages, schedule):
    t = MOBA_BLOCK
    i = pl.program_id(1)
    n_blk = kmean_ref.shape[1]
    n_seq = top_ref.shape[0]
    n_slots, _, page = buf.shape
    pages_per_block = MOBA_BLOCK // page
    blocks_per_seq = n_pages // pages_per_block
    last_page = n_seq * n_pages - 1
    gate_own, gate_prev, gate_far = schedule

    def page_copy(g):
        slot = g % n_slots
        return pltpu.make_async_copy(kpool_ref.at[pt_ref[jnp.minimum(g, last_page)]], buf.at[slot], sem.at[slot])

    @pl.when(jnp.logical_and(pl.program_id(0) == 0, i == 0))
    def _():
        cnt_ref[0] = 0
        ksum[...] = jnp.zeros(ksum.shape, F32)
        for g in range(n_slots):
            page_copy(g).start()

    col = lax.broadcasted_iota(I32, (N_HEADS, LANES), 1)

    def gate_block(b):
        qbd, _ = _block_diag_rows(sq_ref[b // blocks_per_seq])
        scores = None
        for c in range(pages_per_block):
            g = b * pages_per_block + c
            z = jnp.dot(qbd, buf[g % n_slots].astype(BF16), preferred_element_type=F32)
            scores = z if scores is None else scores + z
        return jnp.sum(scores, axis=1, keepdims=True) * (1.0 / MOBA_BLOCK)

    def finish_sequence(seq):
        gate = ksum[...]
        gate = jnp.where(seq % 2 == 1, pltpu.roll(gate, LANES - blocks_per_seq, 1), gate)
        blk = lax.broadcasted_iota(I32, gate.shape, 1)
        rank = jnp.zeros(gate.shape, I32)
        for m in range(blocks_per_seq):
            gm = gate[:, m:m + 1]
            beats = jnp.logical_or(gm > gate, jnp.logical_and(gm == gate, m < blk))
            rank = rank + jnp.where(beats, 1, 0)
        top = jnp.zeros((N_HEADS, LANES), I32)
        for j in range(MOBA_TOPK):
            hit = jnp.logical_and(rank == j, blk < blocks_per_seq)
            idx = jnp.sum(jnp.where(hit, blk, 0), axis=1, keepdims=True)
            top = jnp.where(blk == j, idx, top)
        top_ref[seq] = top

    q = q_ref[0]
    _store_masked_heads(q, qs_ref, SCALE)
    key = lax.broadcasted_iota(I32, (t, t), 0)
    qry = lax.broadcasted_iota(I32, (t, t), 1)
    blk = lax.broadcasted_iota(I32, (n_blk, t), 0)
    lane = lax.broadcasted_iota(I32, (n_blk, LANES), 1)

    for h in range(N_HEADS):
        pair, hh = divmod(h, HEADS_PER_TILE)
        km = kmean_ref[0, :, pair * LANES:(pair + 1) * LANES]
        km = jnp.where(lane // HEAD_DIM == hh, km, 0.0).astype(BF16)
        gate = lax.dot_general(km, q[:, pair * LANES:(pair + 1) * LANES], _NT,
                               preferred_element_type=F32)
        rank = jnp.zeros((n_blk, t), I32)
        for m in range(n_blk):
            gm = gate[m:m + 1, :]
            beats = jnp.logical_or(gm > gate, jnp.logical_and(gm == gate, m < blk))
            rank = rank + jnp.where(beats, (m < i).astype(I32), 0)
        sel = jnp.logical_and(rank < MOBA_TOPK, blk < i)
        mask_ref[h] = jnp.where(sel, 0.0, -jnp.inf)

    ones = jnp.ones((ONES_ROWS, t), BF16)

    def tile(n, bias, shift, first, n_gate):
        start = pl.multiple_of(n * t, t)
        alphas, offsets = [], []
        for h in range(N_HEADS):
            pair = h // HEADS_PER_TILE
            row = pl.ds(h, 1)
            kt = k_ref[0, pl.ds(start, t), pair * LANES:(pair + 1) * LANES]
            s = lax.dot_general(kt, qs_ref[h], _NT, preferred_element_type=F32)
            if bias is not None:
                s = s + bias(h)
            s_ref[h] = s
            m_tile = jnp.max(s, axis=0, keepdims=True)
            if shift is not None:
                m_tile = m_tile + shift(h)
            if first:
                m_new = m_tile
            else:
                m_old = m_ref[row, :]
                m_new = jnp.maximum(m_old, m_tile)
                alphas.append(jnp.exp(m_old - m_new))
            m_ref[row, :] = m_new
            offsets.append(m_new if shift is None else m_new - shift(h))
        done = cnt_ref[0]
        pages = [done * pages_per_block + u for u in range(n_gate * pages_per_block)]
        for g in pages:
            page_copy(g).wait()
        values = []
        for h in range(N_HEADS):
            p = jnp.exp(s_ref[h] - offsets[h]).astype(BF16)
            vt = jnp.concatenate([vt_ref[n, pl.ds(h * HEAD_DIM, HEAD_DIM), :], ones], axis=0)
            pv = jnp.dot(vt, p, preferred_element_type=F32)
            acc_ref[h] = pv if first else alphas[h] * acc_ref[h] + pv
            if h < n_gate:
                values.append(gate_block(done + h))
        for g in pages:
            page_copy(g + n_slots).start()
        if n_gate:
            gates = ksum[...]
            for u, value in enumerate(values):
                gates = jnp.where(col == (done + u) % (2 * blocks_per_seq), value, gates)
            ksum[...] = gates
            cnt_ref[0] = done + n_gate

            @pl.when((done + n_gate) // blocks_per_seq > done // blocks_per_seq)
            def _():
                finish_sequence((done + n_gate) // blocks_per_seq - 1)

    tile(i, lambda h: jnp.where(key <= qry, bias_ref[h, 0], -jnp.inf), None, True, gate_own)

    @pl.when(i >= 1)
    def _():
        n = i - 1
        tile(n, lambda h: bias_ref[h, 1], lambda h: mask_ref[h, pl.ds(n, 1), :], False, gate_prev)

    def far(n, _):
        tile(n, None, lambda h: mask_ref[h, pl.ds(n, 1), :] + table_ref[REL_BUCKETS - 1, h], False, gate_far)
        return 0

    lax.fori_loop(0, jnp.maximum(i - 1, 0), far, 0)

    outs = []
    for h in range(N_HEADS):
        acc = acc_ref[h]
        outs.append(acc[:HEAD_DIM] / acc[HEAD_DIM:HEAD_DIM + 1])
    y_ref[0] = (jnp.concatenate(outs, axis=0).T * _silu(g_ref[0])).astype(BF16)

    @pl.when(jnp.logical_and(pl.program_id(0) == pl.num_programs(0) - 1, i == n_blk - 1))
    def _():
        for g in range(n_slots):
            page_copy(g).wait()


def _moba_prompt(pt_flat, rel_bias, q16, k16, vt16, kmean, bias_tiles, gate, sample_q16, kpool,
                 batch, seq, n_pages):
    n_blk = seq // MOBA_BLOCK
    assert n_blk > MOBA_TOPK
    n_seq = sample_q16.shape[0]
    page = kpool.shape[2]
    assert MOBA_BLOCK % page == 0 and n_pages % (MOBA_BLOCK // page) == 0
    blocks_per_seq = n_pages // (MOBA_BLOCK // page)
    assert MOBA_TOPK <= blocks_per_seq and 2 * blocks_per_seq <= LANES
    schedule = _gate_schedule(batch, n_blk, n_seq * blocks_per_seq)
    n_slots = min(32, n_seq * n_pages)
    tile_spec = pl.BlockSpec((1, MOBA_BLOCK, GROUP_WIDTH), lambda b, i, pt: (b, i, 0))
    whole = lambda shape: pl.BlockSpec(shape, lambda b, i, pt: (0,) * len(shape))
    grid_spec = pltpu.PrefetchScalarGridSpec(
        num_scalar_prefetch=1,
        grid=(batch, n_blk),
        in_specs=[pl.BlockSpec(memory_space=pltpu.SMEM),
                  tile_spec,
                  pl.BlockSpec((1, seq, GROUP_WIDTH), lambda b, i, pt: (b, 0, 0)),
                  pl.BlockSpec((n_blk, GROUP_WIDTH, MOBA_BLOCK), lambda b, i, pt: (b, 0, 0)),
                  pl.BlockSpec((1, n_blk, GROUP_WIDTH), lambda b, i, pt: (b, 0, 0)),
                  whole((N_HEADS, 2, MOBA_BLOCK, MOBA_BLOCK)),
                  tile_spec,
                  whole((n_seq, 1, GROUP_WIDTH)),
                  pl.BlockSpec(memory_space=pl.ANY)],
        out_specs=[tile_spec, whole((n_seq, N_HEADS, LANES))],
        scratch_shapes=[pltpu.VMEM((N_HEADS, MOBA_BLOCK, LANES), BF16),
                        pltpu.VMEM((N_HEADS, n_blk, MOBA_BLOCK), F32),
                        pltpu.VMEM((N_HEADS, MOBA_BLOCK), F32),
                        pltpu.VMEM((N_HEADS, HEAD_DIM + ONES_ROWS, MOBA_BLOCK), F32),
                        pltpu.VMEM((N_HEADS, MOBA_BLOCK, MOBA_BLOCK), F32),
                        pltpu.VMEM((n_slots, GROUP_WIDTH, page), F32),
                        pltpu.SemaphoreType.DMA((n_slots,)),
                        pltpu.VMEM((N_HEADS, LANES), F32),
                        pltpu.SMEM((1,), I32)],
    )
    return pl.pallas_call(
        functools.partial(_moba_prompt_kernel, n_pages=n_pages, schedule=schedule),
        grid_spec=grid_spec,
        out_shape=[jax.ShapeDtypeStruct((batch, seq, GROUP_WIDTH), BF16),
                   jax.ShapeDtypeStruct((n_seq, N_HEADS, LANES), I32)],
        compiler_params=pltpu.CompilerParams(
            dimension_semantics=("arbitrary", "arbitrary"), vmem_limit_bytes=VMEM_LIMIT),
        name="moba_prompt",
    )(pt_flat, rel_bias, q16, k16, vt16, kmean, bias_tiles, gate, sample_q16, kpool)


def _mix_kernel(x_ref, a_ref, b_ref, *rest, gated):
    if gated:
        w_ref, g_ref, y_ref = rest
        ya, yb = a_ref[...], b_ref[...]
    else:
        ga_ref, gb_ref, w_ref, g_ref, y_ref = rest
        ya = (a_ref[...] * _silu(ga_ref[...])).astype(BF16)
        yb = (b_ref[...] * _silu(gb_ref[...])).astype(BF16)
    y = (jnp.dot(ya, w_ref[:GROUP_WIDTH, :], preferred_element_type=F32)
         + jnp.dot(yb, w_ref[GROUP_WIDTH:, :], preferred_element_type=F32))
    ms = jnp.mean(y * y, axis=-1, keepdims=True)
    y_ref[...] = x_ref[...] + y * lax.rsqrt(ms + RMS_EPS) * g_ref[...]


def _mix_out(x2d, a, b, gates, w16, g_post):
    m, d = x2d.shape
    tm = min(m, 256)
    assert m % tm == 0
    row = lambda i: (i, 0)
    wide = pl.BlockSpec((tm, GROUP_WIDTH), row)
    extra = () if gates is None else tuple(gates)
    return pl.pallas_call(
        functools.partial(_mix_kernel, gated=gates is None),
        grid=(m // tm,),
        in_specs=[pl.BlockSpec((tm, d), row), wide, wide] + [wide] * len(extra)
                 + [pl.BlockSpec(w16.shape, lambda i: (0, 0)), pl.BlockSpec((1, d), lambda i: (0, 0))],
        out_specs=pl.BlockSpec((tm, d), row),
        out_shape=jax.ShapeDtypeStruct((m, d), F32),
        compiler_params=pltpu.CompilerParams(dimension_semantics=("arbitrary",),
                                             vmem_limit_bytes=VMEM_LIMIT),
        name="mix_out",
    )(x2d, a, b, *extra, w16, g_post)


def _block_diag_rows(row):
    shape = (N_HEADS, GROUP_WIDTH)
    own = lax.broadcasted_iota(I32, shape, 1) // HEAD_DIM == lax.broadcasted_iota(I32, shape, 0)
    return jnp.where(own, jnp.broadcast_to(row.astype(F32), shape), 0.0).astype(row.dtype), own


def _sb_sample_kernel(pt_ref, q_ref, k1_ref, k2_ref, v1_ref, v2_ref, kpool_ref, vpool_ref, o_ref,
                      kbuf, vbuf, sem, *, n_pages):
    b = pl.program_id(0)
    page = kbuf.shape[1]
    qbd, own = _block_diag_rows(q_ref[0] * SCALE)
    lower = (lax.broadcasted_iota(I32, (page, page), 0)
             > lax.broadcasted_iota(I32, (page, page), 1)).astype(BF16)
    lower2 = jnp.concatenate([lower, lower], axis=0)

    def page_terms(kp):
        z = jnp.dot(qbd, kp.astype(BF16), preferred_element_type=F32)
        sp = _softplus(z)
        lk = -sp
        hi, lo = _split_bf16(lk)
        suffix = jnp.dot(jnp.concatenate([hi, lo], axis=1), lower2, preferred_element_type=F32)
        return z - sp, suffix, jnp.sum(lk, axis=1, keepdims=True)

    def weighted_values(log_sig, after, vp):
        a = jnp.exp(log_sig + after)
        return lax.dot_general(a.astype(BF16), vp.astype(BF16), _NT, preferred_element_type=F32)

    def step(kp, vp, carry, acc):
        log_sig, suffix, total = page_terms(kp)
        return carry + total, acc + weighted_values(log_sig, suffix + carry, vp)

    ls1, suffix1, total1 = page_terms(k1_ref[0])
    ls2, suffix2, total2 = page_terms(k2_ref[0])
    acc = weighted_values(ls1, suffix1, v1_ref[0]) + weighted_values(ls2, suffix2 + total1, v2_ref[0])
    carry = total1 + total2

    def cond(s):
        p, carry, _ = s
        return jnp.logical_and(p >= 0, jnp.max(carry) > SB_EXIT)

    def body(s):
        p, carry, acc = s
        phys = pt_ref[b * n_pages + p]
        ck = pltpu.make_async_copy(kpool_ref.at[phys], kbuf, sem.at[0])
        cv = pltpu.make_async_copy(vpool_ref.at[phys], vbuf, sem.at[1])
        ck.start()
        cv.start()
        ck.wait()
        cv.wait()
        carry, acc = step(kbuf[...], vbuf[...], carry, acc)
        return p - 1, carry, acc

    _, _, acc = lax.while_loop(cond, body, (n_pages - 3, carry, acc))
    o_ref[0] = jnp.sum(jnp.where(own, acc, 0.0), axis=0, keepdims=True)


def _sb_sample(pt_flat, q16, kpool, vpool, n_seq, n_pages):
    page = kpool.shape[2]
    assert n_pages >= 2
    last = lambda off: (lambda b, pt: (pt[b * n_pages + n_pages - off], 0, 0))
    page_spec = lambda off: pl.BlockSpec((1, GROUP_WIDTH, page), last(off))
    grid_spec = pltpu.PrefetchScalarGridSpec(
        num_scalar_prefetch=1,
        grid=(n_seq,),
        in_specs=[pl.BlockSpec((1, 1, GROUP_WIDTH), lambda b, pt: (b, 0, 0)),
                  page_spec(1), page_spec(2), page_spec(1), page_spec(2),
                  pl.BlockSpec(memory_space=pl.ANY), pl.BlockSpec(memory_space=pl.ANY)],
        out_specs=pl.BlockSpec((1, 1, GROUP_WIDTH), lambda b, pt: (b, 0, 0)),
        scratch_shapes=[pltpu.VMEM((GROUP_WIDTH, page), F32), pltpu.VMEM((GROUP_WIDTH, page), F32),
                        pltpu.SemaphoreType.DMA((2,))],
    )
    return pl.pallas_call(
        functools.partial(_sb_sample_kernel, n_pages=n_pages),
        grid_spec=grid_spec,
        out_shape=jax.ShapeDtypeStruct((n_seq, 1, GROUP_WIDTH), F32),
        compiler_params=pltpu.CompilerParams(dimension_semantics=("arbitrary",),
                                             vmem_limit_bytes=VMEM_LIMIT),
        name="sb_sample",
    )(pt_flat, q16, kpool, kpool, vpool, vpool, kpool, vpool)


def _moba_sample_kernel(pt_ref, top_ref, topv_ref, table_ref, q_ref, knew_ref, vnew_ref,
                        kpool_ref, vpool_ref, o_ref, kbuf, vbuf, sem, *, n_pages):
    b = pl.program_id(0)
    n_seq = pl.num_programs(0)
    page = kpool_ref.shape[3]
    pages_per_block = MOBA_BLOCK // page
    n_sel = MOBA_TOPK * MOBA_BLOCK
    past_len = n_pages * page

    def copies(seq, slot):
        out = []
        for h in range(N_HEADS):
            for j in range(MOBA_TOPK):
                blk = top_ref[(seq * N_HEADS + h) * MOBA_TOPK + j]
                for c in range(pages_per_block):
                    phys = pt_ref[seq * n_pages + blk * pages_per_block + c]
                    cols = pl.ds((j * pages_per_block + c) * page, page)
                    out.append(pltpu.make_async_copy(kpool_ref.at[phys, h],
                                                     kbuf.at[slot, h, :, cols], sem.at[slot, 0]))
                    out.append(pltpu.make_async_copy(vpool_ref.at[phys, h],
                                                     vbuf.at[slot, h, :, cols], sem.at[slot, 1]))
        return out

    n_buf = kbuf.shape[0]
    ahead = n_buf - 1
    slot = b % n_buf

    @pl.when(b == 0)
    def _():
        for first in range(min(ahead, n_seq)):
            for c in copies(first, first):
                c.start()

    @pl.when(b + ahead < n_seq)
    def _():
        for c in copies(b + ahead, (b + ahead) % n_buf):
            c.start()

    for c in copies(b, slot):
        c.wait()

    q_f = q_ref[0].astype(F32)
    head = lax.broadcasted_iota(I32, (N_HEADS, n_sel), 0)
    s = jnp.zeros((N_HEADS, n_sel), F32)
    for h in range(N_HEADS):
        q8 = jnp.broadcast_to(q_f[h:h + 1, :] * SCALE, (8, HEAD_DIM)).astype(BF16)
        s_h = jnp.dot(q8, kbuf[slot, h].astype(BF16), preferred_element_type=F32)
        s = jnp.where(head == h, s_h, s)
    pos = lax.broadcasted_iota(I32, (N_HEADS, n_sel), 1)
    which = pos // MOBA_BLOCK
    topv = topv_ref[0]
    blk = jnp.zeros((N_HEADS, n_sel), I32)
    for j in range(MOBA_TOPK):
        blk = jnp.where(which == j, topv[:, j:j + 1], blk)
    rel = past_len - (blk * MOBA_BLOCK + pos % MOBA_BLOCK)
    s = s + _rel_bias(rel, lambda k: table_ref[:, k:k + 1])
    knew = knew_ref[0].astype(BF16).astype(F32)
    own = jnp.sum(q_f * knew, axis=1, keepdims=True) * SCALE + table_ref[:, 0:1]
    m = jnp.maximum(jnp.max(s, axis=1, keepdims=True), own)
    p = jnp.exp(s - m)
    p_own = jnp.exp(own - m)
    denom = jnp.sum(p, axis=1, keepdims=True) + p_own
    p16 = p.astype(BF16)
    head_o = lax.broadcasted_iota(I32, (N_HEADS, HEAD_DIM), 0)
    pv = jnp.zeros((N_HEADS, HEAD_DIM), F32)
    for h in range(N_HEADS):
        pv_h = lax.dot_general(p16, vbuf[slot, h].astype(BF16), _NT, preferred_element_type=F32)
        pv = jnp.where(head_o == h, pv_h, pv)
    vnew = vnew_ref[0].astype(BF16).astype(F32)
    o_ref[0] = (pv + p_own.astype(BF16).astype(F32) * vnew) / denom


def _moba_sample(pt_flat, top, rel_bias, q16, k_new, v_new, kpool, vpool, n_seq, n_pages):
    n_sel = MOBA_TOPK * MOBA_BLOCK
    head_rows = pl.BlockSpec((1, N_HEADS, HEAD_DIM), lambda b, pt, tp: (b, 0, 0))
    whole = lambda a: pl.BlockSpec(a.shape, lambda b, pt, tp: (0,) * a.ndim)
    table_t = rel_bias.T
    grid_spec = pltpu.PrefetchScalarGridSpec(
        num_scalar_prefetch=2,
        grid=(n_seq,),
        in_specs=[pl.BlockSpec((1, N_HEADS, LANES), lambda b, pt, tp: (b, 0, 0)),
                  whole(table_t),
                  head_rows, head_rows, head_rows,
                  pl.BlockSpec(memory_space=pl.ANY), pl.BlockSpec(memory_space=pl.ANY)],
        out_specs=head_rows,
        scratch_shapes=[pltpu.VMEM((SAMPLE_BUFFERS, N_HEADS, HEAD_DIM, n_sel), F32),
                        pltpu.VMEM((SAMPLE_BUFFERS, N_HEADS, HEAD_DIM, n_sel), F32),
                        pltpu.SemaphoreType.DMA((SAMPLE_BUFFERS, 2))],
    )
    return pl.pallas_call(
        functools.partial(_moba_sample_kernel, n_pages=n_pages),
        grid_spec=grid_spec,
        out_shape=jax.ShapeDtypeStruct((n_seq, N_HEADS, HEAD_DIM), F32),
        compiler_params=pltpu.CompilerParams(dimension_semantics=("arbitrary",),
                                             vmem_limit_bytes=VMEM_LIMIT),
        name="moba_sample",
    )(pt_flat, top[:, :, :MOBA_TOPK].reshape(-1), top, table_t, q16, k_new, v_new, kpool, vpool)


def kernel(x_prompt, x_sample, cache_k_sb, cache_v_sb, cache_k_moba, cache_v_moba, page_table,
           w_in, w_out, g_pre, g_post, rel_bias):
    depth, d_model, d_in = w_in.shape
    assert depth == 1 and d_in == 8 * GROUP_WIDTH and w_out.shape[1] == 2 * GROUP_WIDTH
    batch, seq, _ = x_prompt.shape
    n_seq, dec_seq, _ = x_sample.shape
    assert dec_seq == 1 and seq % MOBA_BLOCK == 0
    n_pool, page = cache_k_sb.shape[1:3]
    n_pages = page_table.shape[1]

    w_in16 = w_in[0].astype(BF16)
    w_out16 = w_out[0].astype(BF16)
    pt_flat = page_table.reshape(-1).astype(I32)
    pool4 = lambda c: jnp.transpose(c[0], (0, 2, 3, 1))
    pool3 = lambda c: pool4(c).reshape(n_pool, GROUP_WIDTH, page)

    (q_a, g_a, q_b, g_b, kt_a, vt_a, kt_b, vt_b, ka16, kb16, vta16, vtb16, kmean_b) = _project(
        x_prompt.reshape(batch * seq, d_model), g_pre, w_in16, True, seq)
    sq_a, sg_a, sq_b, sg_b, skt_a, svt_a, skt_b, svt_b, sk_b, sv_b = _project(
        x_sample.reshape(n_seq, d_model), g_pre, w_in16, False, n_seq)

    as_seq = lambda a: a.reshape(batch, seq, GROUP_WIDTH)
    y_a = _sb_prompt(as_seq(q_a), as_seq(ka16), vta16, as_seq(g_a), batch, seq)
    y_b, top = _moba_prompt(pt_flat, rel_bias, as_seq(q_b), as_seq(kb16), vtb16,
                            kmean_b.reshape(batch, seq // MOBA_BLOCK, GROUP_WIDTH),
                            _bias_tiles(rel_bias), as_seq(g_b),
                            sq_b.reshape(n_seq, 1, GROUP_WIDTH), pool3(cache_k_moba), batch, seq, n_pages)
    y_prompt = _mix_out(x_prompt.reshape(batch * seq, d_model), y_a.reshape(batch * seq, GROUP_WIDTH),
                        y_b.reshape(batch * seq, GROUP_WIDTH), None, w_out16, g_post)

    so_a = _sb_sample(pt_flat, sq_a.reshape(n_seq, 1, GROUP_WIDTH), pool3(cache_k_sb), pool3(cache_v_sb),
                      n_seq, n_pages)
    heads = lambda a: a.reshape(n_seq, N_HEADS, HEAD_DIM)
    so_b = _moba_sample(pt_flat, top, rel_bias, heads(sq_b), heads(sk_b), heads(sv_b),
                        pool4(cache_k_moba), pool4(cache_v_moba), n_seq, n_pages)
    y_sample = _mix_out(x_sample.reshape(n_seq, d_model), so_a.reshape(n_seq, GROUP_WIDTH),
                        so_b.reshape(n_seq, GROUP_WIDTH), (sg_a, sg_b), w_out16, g_post)

    kv = lambda a: jnp.transpose(a.reshape(a.shape[0], N_HEADS, HEAD_DIM, a.shape[2]), (0, 3, 1, 2))[None]
    kv_s = lambda a: jnp.transpose(kv(a), (0, 2, 1, 3, 4))
    return (y_prompt.reshape(batch, seq, d_model), y_sample.reshape(n_seq, 1, d_model),
            kv(kt_a), kv(vt_a), kv(kt_b), kv(vt_b),
            kv_s(skt_a), kv_s(svt_a), kv_s(skt_b), kv_s(svt_b))
```

```python
import functools
import math

import jax
import jax.numpy as jnp
from jax import lax
from jax.experimental import pallas as pl
from jax.experimental.pallas import tpu as pltpu

F32 = jnp.float32
BF16 = jnp.bfloat16
I32 = jnp.int32

HEAD_DIM = 64
LANES = 128
HEADS_PER_TILE = LANES // HEAD_DIM
GROUP_WIDTH = 512
N_HEADS = GROUP_WIDTH // HEAD_DIM
N_PAIRS = N_HEADS // HEADS_PER_TILE
SB_BLOCK = 128
MOBA_BLOCK = 256
MOBA_TOPK = 3
ONES_ROWS = 16
SAMPLE_BUFFERS = 3
REL_BUCKETS = 32
REL_MAX_DIST = 128
RMS_EPS = 1e-6
SCALE = HEAD_DIM ** -0.5
SB_EXIT = -110.0
VMEM_LIMIT = 56 * 1024 * 1024

_NT = (((1,), (1,)), ((), ()))


def _softplus(z):
    return jnp.maximum(z, 0.0) + jnp.log(1.0 + jnp.exp(-jnp.abs(z)))


def _split_bf16(x):
    hi = x.astype(BF16)
    lo = (x - hi.astype(F32)).astype(BF16)
    return hi, lo


def _rel_bias(rel, table):
    max_exact = REL_BUCKETS // 2
    n = jnp.maximum(rel, 0)
    nf = jnp.maximum(n, 1).astype(F32)
    large = max_exact + (jnp.log(nf / max_exact) / math.log(REL_MAX_DIST / max_exact)
                         * (REL_BUCKETS - max_exact)).astype(I32)
    large = jnp.minimum(large, REL_BUCKETS - 1)
    bucket = jnp.where(n < max_exact, n, large)
    out = jnp.broadcast_to(table(REL_BUCKETS - 1), rel.shape).astype(F32)
    for k in range(REL_BUCKETS - 1):
        out = jnp.where(bucket == k, table(k), out)
    return out


def _silu(g):
    return g * jax.nn.sigmoid(g)


def _store_masked_heads(q, qs_ref, scale):
    rows = q.shape[0]
    lane = lax.broadcasted_iota(I32, (rows, LANES), 1)
    for pair in range(N_PAIRS):
        q32 = q[:, pair * LANES:(pair + 1) * LANES].astype(F32) * scale
        for hh in range(HEADS_PER_TILE):
            qs_ref[pair * HEADS_PER_TILE + hh] = jnp.where(lane // HEAD_DIM == hh, q32, 0.0).astype(BF16)


def _proj_kernel(x_ref, g_ref, w_ref, *outs, prompt):
    x = x_ref[...]
    ms = jnp.mean(x * x, axis=-1, keepdims=True)
    h = (x * lax.rsqrt(ms + RMS_EPS) * g_ref[...]).astype(BF16)

    def group(c):
        return jnp.dot(h, w_ref[:, c * GROUP_WIDTH:(c + 1) * GROUP_WIDTH], preferred_element_type=F32)

    q_a, g_a, q_b, g_b, kt_a, vt_a, kt_b, vt_b = outs[:8]
    q_a[...] = group(0).astype(BF16)
    ka = group(1)
    va_t = group(2).T
    g_a[...] = group(3)
    q_b[...] = group(4).astype(BF16)
    kb = group(5)
    vb = group(6)
    vb_t = vb.T
    g_b[...] = group(7)
    kt_a[0] = ka.T
    vt_a[0] = va_t
    kt_b[0] = kb.T
    vt_b[0] = vb_t
    if prompt:
        ka16, kb16, vta16, vtb16, kmean = outs[8:]
        ka16[...] = ka.astype(BF16)
        kb16[...] = kb.astype(BF16)
        va16 = va_t.astype(BF16)
        for c in range(vta16.shape[0]):
            vta16[c] = va16[:, c * SB_BLOCK:(c + 1) * SB_BLOCK]
        vtb16[0] = vb_t.astype(BF16)
        kmean[0] = jnp.mean(kb, axis=0, keepdims=True)
    else:
        k_b, v_b = outs[8:]
        k_b[...] = kb
        v_b[...] = vb


def _project(x2d, g_pre, w16, prompt, seq):
    m, d = x2d.shape
    n_groups = w16.shape[1] // GROUP_WIDTH
    assert n_groups == 8
    tm = MOBA_BLOCK if prompt else m
    assert m % seq == 0 and seq % tm == 0
    per_seq = seq // tm
    row = lambda i: (i, 0)
    wide = pl.BlockSpec((tm, GROUP_WIDTH), row)
    tall = pl.BlockSpec((1, GROUP_WIDTH, tm), lambda i: (i // per_seq, 0, i % per_seq))
    wide_shape = lambda dt: jax.ShapeDtypeStruct((m, GROUP_WIDTH), dt)
    tall_shape = jax.ShapeDtypeStruct((m // seq, GROUP_WIDTH, seq), F32)
    out_shape = [wide_shape(BF16), wide_shape(F32), wide_shape(BF16), wide_shape(F32)] + [tall_shape] * 4
    out_specs = [wide] * 4 + [tall] * 4
    if prompt:
        per = tm // SB_BLOCK
        out_shape += [
            wide_shape(BF16), wide_shape(BF16),
            jax.ShapeDtypeStruct((m // SB_BLOCK, GROUP_WIDTH, SB_BLOCK), BF16),
            jax.ShapeDtypeStruct((m // tm, GROUP_WIDTH, tm), BF16),
            jax.ShapeDtypeStruct((m // tm, 1, GROUP_WIDTH), F32),
        ]
        out_specs += [
            wide, wide,
            pl.BlockSpec((per, GROUP_WIDTH, SB_BLOCK), lambda i: (i, 0, 0)),
            pl.BlockSpec((1, GROUP_WIDTH, tm), lambda i: (i, 0, 0)),
            pl.BlockSpec((1, 1, GROUP_WIDTH), lambda i: (i, 0, 0)),
        ]
    else:
        out_shape += [wide_shape(F32), wide_shape(F32)]
        out_specs += [wide, wide]
    return pl.pallas_call(
        functools.partial(_proj_kernel, prompt=prompt),
        grid=(m // tm,),
        in_specs=[pl.BlockSpec((tm, d), row),
                  pl.BlockSpec((1, d), lambda i: (0, 0)),
                  pl.BlockSpec(w16.shape, lambda i: (0, 0))],
        out_specs=out_specs,
        out_shape=out_shape,
        compiler_params=pltpu.CompilerParams(dimension_semantics=("arbitrary",),
                                             vmem_limit_bytes=VMEM_LIMIT),
        name="proj_prompt" if prompt else "proj_sample",
    )(x2d, g_pre, w16)


def _sb_prompt_kernel(q_ref, k_ref, vt_ref, g_ref, y_ref, qs_ref, acc_ref, z_ref, cs_ref):
    t = SB_BLOCK
    i = pl.program_id(1)
    _store_masked_heads(q_ref[0], qs_ref, SCALE)
    key = lax.broadcasted_iota(I32, (t, t), 0)
    qry = lax.broadcasted_iota(I32, (t, t), 1)
    causal = key < qry
    upper = (qry > key).astype(BF16)
    upper2 = jnp.concatenate([upper, upper], axis=1)

    def tile(j, carry, diag):
        start = pl.multiple_of(j * t, t)
        for h in range(N_HEADS):
            pair = h // HEADS_PER_TILE
            kt = k_ref[0, pl.ds(start, t), pair * LANES:(pair + 1) * LANES]
            z_ref[h] = lax.dot_general(kt, qs_ref[h], _NT, preferred_element_type=F32)
        new_carry = []
        for h in range(N_HEADS):
            z = z_ref[h]
            sp = _softplus(z)
            lk = -sp
            if diag:
                lk = jnp.where(causal, lk, 0.0)
            z_ref[h] = z - sp
            hi, lo = _split_bf16(lk)
            cs_ref[h] = jnp.dot(upper2, jnp.concatenate([hi, lo], axis=0), preferred_element_type=F32)
            new_carry.append(carry[h] + jnp.sum(lk, axis=0, keepdims=True))
        for h in range(N_HEADS):
            a = jnp.exp(z_ref[h] + (cs_ref[h] + carry[h]))
            if diag:
                a = jnp.where(causal, a, 0.0)
            vt = vt_ref[j, h * HEAD_DIM:(h + 1) * HEAD_DIM, :]
            pv = jnp.dot(vt, a.astype(BF16), preferred_element_type=F32)
            rows = pl.ds(h * HEAD_DIM, HEAD_DIM)
            if diag:
                acc_ref[rows, :] = pv
            else:
                acc_ref[rows, :] += pv
        return tuple(new_carry)

    carry = tile(i, tuple(jnp.zeros((1, t), F32) for _ in range(N_HEADS)), True)

    def cond(s):
        j, carry = s
        top = carry[0]
        for c in carry[1:]:
            top = jnp.maximum(top, c)
        return jnp.logical_and(j >= 0, jnp.max(top) > SB_EXIT)

    def body(s):
        j, carry = s
        return j - 1, tile(j, carry, False)

    lax.while_loop(cond, body, (i - 1, carry))
    y_ref[0] = (acc_ref[...].T * _silu(g_ref[0])).astype(BF16)


def _sb_prompt(q16, k16, vt16, gate, batch, seq):
    n_blk = seq // SB_BLOCK
    tile_spec = pl.BlockSpec((1, SB_BLOCK, GROUP_WIDTH), lambda b, i: (b, i, 0))
    grid_spec = pl.GridSpec(
        grid=(batch, n_blk),
        in_specs=[tile_spec,
                  pl.BlockSpec((1, seq, GROUP_WIDTH), lambda b, i: (b, 0, 0)),
                  pl.BlockSpec((n_blk, GROUP_WIDTH, SB_BLOCK), lambda b, i: (b, 0, 0)),
                  tile_spec],
        out_specs=tile_spec,
        scratch_shapes=[pltpu.VMEM((N_HEADS, SB_BLOCK, LANES), BF16),
                        pltpu.VMEM((GROUP_WIDTH, SB_BLOCK), F32),
                        pltpu.VMEM((N_HEADS, SB_BLOCK, SB_BLOCK), F32),
                        pltpu.VMEM((N_HEADS, SB_BLOCK, SB_BLOCK), F32)],
    )
    return pl.pallas_call(
        _sb_prompt_kernel,
        grid_spec=grid_spec,
        out_shape=jax.ShapeDtypeStruct((batch, seq, GROUP_WIDTH), BF16),
        compiler_params=pltpu.CompilerParams(
            dimension_semantics=("arbitrary", "arbitrary"), vmem_limit_bytes=VMEM_LIMIT),
        name="sb_prompt",
    )(q16, k16, vt16, gate)


def _bias_tile_kernel(table_ref, o_ref):
    h = pl.program_id(0)
    t = MOBA_BLOCK
    key = lax.broadcasted_iota(I32, (t, t), 0)
    qry = lax.broadcasted_iota(I32, (t, t), 1)
    for c in range(2):
        o_ref[0, c] = _rel_bias(c * t + qry - key, lambda k: table_ref[k, h])


def _bias_tiles(rel_bias):
    n_heads = rel_bias.shape[1]
    return pl.pallas_call(
        _bias_tile_kernel,
        grid=(n_heads,),
        in_specs=[pl.BlockSpec(memory_space=pltpu.SMEM)],
        out_specs=pl.BlockSpec((1, 2, MOBA_BLOCK, MOBA_BLOCK), lambda h: (h, 0, 0, 0)),
        out_shape=jax.ShapeDtypeStruct((n_heads, 2, MOBA_BLOCK, MOBA_BLOCK), F32),
        name="moba_bias_tiles",
    )(rel_bias)


def _gate_schedule(batch, n_blk, total_blocks):
    n_own = batch * n_blk
    n_prev = batch * (n_blk - 1)
    n_far = batch * (n_blk - 1) * (n_blk - 2) // 2
    for far in range(N_HEADS, -1, -1):
        for prev in range(N_HEADS, -1, -1):
            rest = total_blocks - far * n_far - prev * n_prev
            if rest >= 0 and rest % n_own == 0 and rest // n_own <= N_HEADS:
                return rest // n_own, prev, far
    raise ValueError("no gating schedule for these shapes")


def _moba_prompt_kernel(pt_ref, table_ref, q_ref, k_ref, vt_ref, kmean_ref, bias_ref, g_ref, sq_ref,
                        kpool_ref, y_ref, top_ref, qs_ref, mask_ref, m_ref, acc_ref, s_ref,
                        buf, sem, ksum, cnt_ref, *, n_pages, schedule):
    t = MOBA_BLOCK
    i = pl.program_id(1)
    n_blk = kmean_ref.shape[1]
    n_seq = top_ref.shape[0]
    n_slots, _, page = buf.shape
    pages_per_block = MOBA_BLOCK // page
    blocks_per_seq = n_pages // pages_per_block
    last_page = n_seq * n_pages - 1
    gate_own, gate_prev, gate_far = schedule

    def page_copy(g):
        slot = g % n_slots
        return pltpu.make_async_copy(kpool_ref.at[pt_ref[jnp.minimum(g, last_page)]], buf.at[slot], sem.at[slot])

    @pl.when(jnp.logical_and(pl.program_id(0) == 0, i == 0))
    def _():
        cnt_ref[0] = 0
        ksum[...] = jnp.zeros(ksum.shape, F32)
        for g in range(n_slots):
            page_copy(g).start()

    col = lax.broadcasted_iota(I32, (N_HEADS, LANES), 1)

    def gate_block(b):
        qbd, _ = _block_diag_rows(sq_ref[b // blocks_per_seq])
        folded = None
        for c in range(pages_per_block):
            x = buf[(b * pages_per_block + c) % n_slots]
            folded = x if folded is None else folded + x
        scores = jnp.dot(qbd, folded.astype(BF16), preferred_element_type=F32)
        return jnp.sum(scores, axis=1, keepdims=True) * (1.0 / MOBA_BLOCK)

    def finish_sequence(seq):
        gate = ksum[...]
        gate = jnp.where(seq % 2 == 1, pltpu.roll(gate, LANES - blocks_per_seq, 1), gate)
        blk = lax.broadcasted_iota(I32, gate.shape, 1)
        rank = jnp.zeros(gate.shape, I32)
        for m in range(blocks_per_seq):
            gm = gate[:, m:m + 1]
            beats = jnp.logical_or(gm > gate, jnp.logical_and(gm == gate, m < blk))
            rank = rank + jnp.where(beats, 1, 0)
        top = jnp.zeros((N_HEADS, LANES), I32)
        for j in range(MOBA_TOPK):
            hit = jnp.logical_and(rank == j, blk < blocks_per_seq)
            idx = jnp.sum(jnp.where(hit, blk, 0), axis=1, keepdims=True)
            top = jnp.where(blk == j, idx, top)
        top_ref[seq] = top

    q = q_ref[0]
    _store_masked_heads(q, qs_ref, SCALE)
    key = lax.broadcasted_iota(I32, (t, t), 0)
    qry = lax.broadcasted_iota(I32, (t, t), 1)
    blk = lax.broadcasted_iota(I32, (n_blk, t), 0)
    lane = lax.broadcasted_iota(I32, (n_blk, LANES), 1)

    for h in range(N_HEADS):
        pair, hh = divmod(h, HEADS_PER_TILE)
        km = kmean_ref[0, :, pair * LANES:(pair + 1) * LANES]
        km = jnp.where(lane // HEAD_DIM == hh, km, 0.0).astype(BF16)
        gate = lax.dot_general(km, q[:, pair * LANES:(pair + 1) * LANES], _NT,
                               preferred_element_type=F32)
        rank = jnp.zeros((n_blk, t), I32)
        for m in range(n_blk):
            gm = gate[m:m + 1, :]
            beats = jnp.logical_or(gm > gate, jnp.logical_and(gm == gate, m < blk))
            rank = rank + jnp.where(beats, (m < i).astype(I32), 0)
        sel = jnp.logical_and(rank < MOBA_TOPK, blk < i)
        mask_ref[h] = jnp.where(sel, 0.0, -jnp.inf)

    ones = jnp.ones((ONES_ROWS, t), BF16)

    def tile(n, bias, shift, first, n_gate):
        start = pl.multiple_of(n * t, t)
        alphas, offsets = [], []
        for h in range(N_HEADS):
            pair = h // HEADS_PER_TILE
            row = pl.ds(h, 1)
            kt = k_ref[0, pl.ds(start, t), pair * LANES:(pair + 1) * LANES]
            s = lax.dot_general(kt, qs_ref[h], _NT, preferred_element_type=F32)
            if bias is not None:
                s = s + bias(h)
            s_ref[h] = s
            m_tile = jnp.max(s, axis=0, keepdims=True)
            if shift is not None:
                m_tile = m_tile + shift(h)
            if first:
                m_new = m_tile
            else:
                m_old = m_ref[row, :]
                m_new = jnp.maximum(m_old, m_tile)
                alphas.append(jnp.exp(m_old - m_new))
            m_ref[row, :] = m_new
            offsets.append(m_new if shift is None else m_new - shift(h))
        done = cnt_ref[0]
        pages = [done * pages_per_block + u for u in range(n_gate * pages_per_block)]
        for g in pages:
            page_copy(g).wait()
        values = []
        for h in range(N_HEADS):
            p = jnp.exp(s_ref[h] - offsets[h]).astype(BF16)
            vt = jnp.concatenate([vt_ref[n, pl.ds(h * HEAD_DIM, HEAD_DIM), :], ones], axis=0)
            pv = jnp.dot(vt, p, preferred_element_type=F32)
            acc_ref[h] = pv if first else alphas[h] * acc_ref[h] + pv
            if h < n_gate:
                values.append(gate_block(done + h))
        for g in pages:
            page_copy(g + n_slots).start()
        if n_gate:
            gates = ksum[...]
            for u, value in enumerate(values):
                gates = jnp.where(col == (done + u) % (2 * blocks_per_seq), value, gates)
            ksum[...] = gates
            cnt_ref[0] = done + n_gate

            @pl.when((done + n_gate) // blocks_per_seq > done // blocks_per_seq)
            def _():
                finish_sequence((done + n_gate) // blocks_per_seq - 1)

    tile(i, lambda h: jnp.where(key <= qry, bias_ref[h, 0], -jnp.inf), None, True, gate_own)

    @pl.when(i >= 1)
    def _():
        n = i - 1
        tile(n, lambda h: bias_ref[h, 1], lambda h: mask_ref[h, pl.ds(n, 1), :], False, gate_prev)

    def far(n, _):
        tile(n, None, lambda h: mask_ref[h, pl.ds(n, 1), :] + table_ref[REL_BUCKETS - 1, h], False, gate_far)
        return 0

    lax.fori_loop(0, jnp.maximum(i - 1, 0), far, 0)

    outs = []
    for h in range(N_HEADS):
        acc = acc_ref[h]
        outs.append(acc[:HEAD_DIM] / acc[HEAD_DIM:HEAD_DIM + 1])
    y_ref[0] = (jnp.concatenate(outs, axis=0).T * _silu(g_ref[0])).astype(BF16)

    @pl.when(jnp.logical_and(pl.program_id(0) == pl.num_programs(0) - 1, i == n_blk - 1))
    def _():
        for g in range(n_slots):
            page_copy(g).wait()


def _moba_prompt(pt_flat, rel_bias, q16, k16, vt16, kmean, bias_tiles, gate, sample_q16, kpool,
                 batch, seq, n_pages):
    n_blk = seq // MOBA_BLOCK
    assert n_blk > MOBA_TOPK
    n_seq = sample_q16.shape[0]
    page = kpool.shape[2]
    assert MOBA_BLOCK % page == 0 and n_pages % (MOBA_BLOCK // page) == 0
    blocks_per_seq = n_pages // (MOBA_BLOCK // page)
    assert MOBA_TOPK <= blocks_per_seq and 2 * blocks_per_seq <= LANES
    schedule = _gate_schedule(batch, n_blk, n_seq * blocks_per_seq)
    n_slots = min(32, n_seq * n_pages)
    tile_spec = pl.BlockSpec((1, MOBA_BLOCK, GROUP_WIDTH), lambda b, i, pt: (b, i, 0))
    whole = lambda shape: pl.BlockSpec(shape, lambda b, i, pt: (0,) * len(shape))
    grid_spec = pltpu.PrefetchScalarGridSpec(
        num_scalar_prefetch=1,
        grid=(batch, n_blk),
        in_specs=[pl.BlockSpec(memory_space=pltpu.SMEM),
                  tile_spec,
                  pl.BlockSpec((1, seq, GROUP_WIDTH), lambda b, i, pt: (b, 0, 0)),
                  pl.BlockSpec((n_blk, GROUP_WIDTH, MOBA_BLOCK), lambda b, i, pt: (b, 0, 0)),
                  pl.BlockSpec((1, n_blk, GROUP_WIDTH), lambda b, i, pt: (b, 0, 0)),
                  whole((N_HEADS, 2, MOBA_BLOCK, MOBA_BLOCK)),
                  tile_spec,
                  whole((n_seq, 1, GROUP_WIDTH)),
                  pl.BlockSpec(memory_space=pl.ANY)],
        out_specs=[tile_spec, whole((n_seq, N_HEADS, LANES))],
        scratch_shapes=[pltpu.VMEM((N_HEADS, MOBA_BLOCK, LANES), BF16),
                        pltpu.VMEM((N_HEADS, n_blk, MOBA_BLOCK), F32),
                        pltpu.VMEM((N_HEADS, MOBA_BLOCK), F32),
                        pltpu.VMEM((N_HEADS, HEAD_DIM + ONES_ROWS, MOBA_BLOCK), F32),
                        pltpu.VMEM((N_HEADS, MOBA_BLOCK, MOBA_BLOCK), F32),
                        pltpu.VMEM((n_slots, GROUP_WIDTH, page), F32),
                        pltpu.SemaphoreType.DMA((n_slots,)),
                        pltpu.VMEM((N_HEADS, LANES), F32),
                        pltpu.SMEM((1,), I32)],
    )
    return pl.pallas_call(
        functools.partial(_moba_prompt_kernel, n_pages=n_pages, schedule=schedule),
        grid_spec=grid_spec,
        out_shape=[jax.ShapeDtypeStruct((batch, seq, GROUP_WIDTH), BF16),
                   jax.ShapeDtypeStruct((n_seq, N_HEADS, LANES), I32)],
        compiler_params=pltpu.CompilerParams(
            dimension_semantics=("arbitrary", "arbitrary"), vmem_limit_bytes=VMEM_LIMIT),
        name="moba_prompt",
    )(pt_flat, rel_bias, q16, k16, vt16, kmean, bias_tiles, gate, sample_q16, kpool)


def _mix_kernel(x_ref, a_ref, b_ref, *rest, gated):
    if gated:
        w_ref, g_ref, y_ref = rest
        ya, yb = a_ref[...], b_ref[...]
    else:
        ga_ref, gb_ref, w_ref, g_ref, y_ref = rest
        ya = (a_ref[...] * _silu(ga_ref[...])).astype(BF16)
        yb = (b_ref[...] * _silu(gb_ref[...])).astype(BF16)
    y = (jnp.dot(ya, w_ref[:GROUP_WIDTH, :], preferred_element_type=F32)
         + jnp.dot(yb, w_ref[GROUP_WIDTH:, :], preferred_element_type=F32))
    ms = jnp.mean(y * y, axis=-1, keepdims=True)
    y_ref[...] = x_ref[...] + y * lax.rsqrt(ms + RMS_EPS) * g_ref[...]


def _mix_out(x2d, a, b, gates, w16, g_post):
    m, d = x2d.shape
    tm = min(m, 512)
    assert m % tm == 0
    row = lambda i: (i, 0)
    wide = pl.BlockSpec((tm, GROUP_WIDTH), row)
    extra = () if gates is None else tuple(gates)
    return pl.pallas_call(
        functools.partial(_mix_kernel, gated=gates is None),
        grid=(m // tm,),
        in_specs=[pl.BlockSpec((tm, d), row), wide, wide] + [wide] * len(extra)
                 + [pl.BlockSpec(w16.shape, lambda i: (0, 0)), pl.BlockSpec((1, d), lambda i: (0, 0))],
        out_specs=pl.BlockSpec((tm, d), row),
        out_shape=jax.ShapeDtypeStruct((m, d), F32),
        compiler_params=pltpu.CompilerParams(dimension_semantics=("arbitrary",),
                                             vmem_limit_bytes=VMEM_LIMIT),
        name="mix_out",
    )(x2d, a, b, *extra, w16, g_post)


def _block_diag_rows(row):
    shape = (N_HEADS, GROUP_WIDTH)
    own = lax.broadcasted_iota(I32, shape, 1) // HEAD_DIM == lax.broadcasted_iota(I32, shape, 0)
    return jnp.where(own, jnp.broadcast_to(row.astype(F32), shape), 0.0).astype(row.dtype), own


def _sb_sample_kernel(pt_ref, q_ref, k1_ref, k2_ref, v1_ref, v2_ref, kpool_ref, vpool_ref, o_ref,
                      kbuf, vbuf, sem, *, n_pages):
    b = pl.program_id(0)
    page = kbuf.shape[1]
    qbd, own = _block_diag_rows(q_ref[0] * SCALE)
    lower = (lax.broadcasted_iota(I32, (page, page), 0)
             > lax.broadcasted_iota(I32, (page, page), 1)).astype(BF16)
    lower2 = jnp.concatenate([lower, lower], axis=0)

    def page_terms(kp):
        z = jnp.dot(qbd, kp.astype(BF16), preferred_element_type=F32)
        sp = _softplus(z)
        lk = -sp
        hi, lo = _split_bf16(lk)
        suffix = jnp.dot(jnp.concatenate([hi, lo], axis=1), lower2, preferred_element_type=F32)
        return z - sp, suffix, jnp.sum(lk, axis=1, keepdims=True)

    def weighted_values(log_sig, after, vp):
        a = jnp.exp(log_sig + after)
        return lax.dot_general(a.astype(BF16), vp.astype(BF16), _NT, preferred_element_type=F32)

    def step(kp, vp, carry, acc):
        log_sig, suffix, total = page_terms(kp)
        return carry + total, acc + weighted_values(log_sig, suffix + carry, vp)

    ls1, suffix1, total1 = page_terms(k1_ref[0])
    ls2, suffix2, total2 = page_terms(k2_ref[0])
    acc = weighted_values(ls1, suffix1, v1_ref[0]) + weighted_values(ls2, suffix2 + total1, v2_ref[0])
    carry = total1 + total2

    def cond(s):
        p, carry, _ = s
        return jnp.logical_and(p >= 0, jnp.max(carry) > SB_EXIT)

    def body(s):
        p, carry, acc = s
        phys = pt_ref[b * n_pages + p]
        ck = pltpu.make_async_copy(kpool_ref.at[phys], kbuf, sem.at[0])
        cv = pltpu.make_async_copy(vpool_ref.at[phys], vbuf, sem.at[1])
        ck.start()
        cv.start()
        ck.wait()
        cv.wait()
        carry, acc = step(kbuf[...], vbuf[...], carry, acc)
        return p - 1, carry, acc

    _, _, acc = lax.while_loop(cond, body, (n_pages - 3, carry, acc))
    o_ref[0] = jnp.sum(jnp.where(own, acc, 0.0), axis=0, keepdims=True)


def _sb_sample(pt_flat, q16, kpool, vpool, n_seq, n_pages):
    page = kpool.shape[2]
    assert n_pages >= 2
    last = lambda off: (lambda b, pt: (pt[b * n_pages + n_pages - off], 0, 0))
    page_spec = lambda off: pl.BlockSpec((1, GROUP_WIDTH, page), last(off))
    grid_spec = pltpu.PrefetchScalarGridSpec(
        num_scalar_prefetch=1,
        grid=(n_seq,),
        in_specs=[pl.BlockSpec((1, 1, GROUP_WIDTH), lambda b, pt: (b, 0, 0)),
                  page_spec(1), page_spec(2), page_spec(1), page_spec(2),
                  pl.BlockSpec(memory_space=pl.ANY), pl.BlockSpec(memory_space=pl.ANY)],
        out_specs=pl.BlockSpec((1, 1, GROUP_WIDTH), lambda b, pt: (b, 0, 0)),
        scratch_shapes=[pltpu.VMEM((GROUP_WIDTH, page), F32), pltpu.VMEM((GROUP_WIDTH, page), F32),
                        pltpu.SemaphoreType.DMA((2,))],
    )
    return pl.pallas_call(
        functools.partial(_sb_sample_kernel, n_pages=n_pages),
        grid_spec=grid_spec,
        out_shape=jax.ShapeDtypeStruct((n_seq, 1, GROUP_WIDTH), F32),
        compiler_params=pltpu.CompilerParams(dimension_semantics=("arbitrary",),
                                             vmem_limit_bytes=VMEM_LIMIT),
        name="sb_sample",
    )(pt_flat, q16, kpool, kpool, vpool, vpool, kpool, vpool)


def _moba_sample_kernel(pt_ref, top_ref, topv_ref, table_ref, q_ref, knew_ref, vnew_ref,
                        kpool_ref, vpool_ref, o_ref, kbuf, vbuf, sem, *, n_pages):
    b = pl.program_id(0)
    n_seq = pl.num_programs(0)
    page = kpool_ref.shape[3]
    pages_per_block = MOBA_BLOCK // page
    n_sel = MOBA_TOPK * MOBA_BLOCK
    past_len = n_pages * page

    def copies(seq, slot):
        out = []
        for h in range(N_HEADS):
            for j in range(MOBA_TOPK):
                blk = top_ref[(seq * N_HEADS + h) * MOBA_TOPK + j]
                for c in range(pages_per_block):
                    phys = pt_ref[seq * n_pages + blk * pages_per_block + c]
                    cols = pl.ds((j * pages_per_block + c) * page, page)
                    out.append(pltpu.make_async_copy(kpool_ref.at[phys, h],
                                                     kbuf.at[slot, h, :, cols], sem.at[slot, 0]))
                    out.append(pltpu.make_async_copy(vpool_ref.at[phys, h],
                                                     vbuf.at[slot, h, :, cols], sem.at[slot, 1]))
        return out

    n_buf = kbuf.shape[0]
    ahead = n_buf - 1
    slot = b % n_buf

    @pl.when(b == 0)
    def _():
        for first in range(min(ahead, n_seq)):
            for c in copies(first, first):
                c.start()

    @pl.when(b + ahead < n_seq)
    def _():
        for c in copies(b + ahead, (b + ahead) % n_buf):
            c.start()

    for c in copies(b, slot):
        c.wait()

    q_f = q_ref[0].astype(F32)
    head = lax.broadcasted_iota(I32, (N_HEADS, n_sel), 0)
    s = jnp.zeros((N_HEADS, n_sel), F32)
    for h in range(N_HEADS):
        q8 = jnp.broadcast_to(q_f[h:h + 1, :] * SCALE, (8, HEAD_DIM)).astype(BF16)
        s_h = jnp.dot(q8, kbuf[slot, h].astype(BF16), preferred_element_type=F32)
        s = jnp.where(head == h, s_h, s)
    pos = lax.broadcasted_iota(I32, (N_HEADS, n_sel), 1)
    which = pos // MOBA_BLOCK
    topv = topv_ref[0]
    blk = jnp.zeros((N_HEADS, n_sel), I32)
    for j in range(MOBA_TOPK):
        blk = jnp.where(which == j, topv[:, j:j + 1], blk)
    rel = past_len - (blk * MOBA_BLOCK + pos % MOBA_BLOCK)
    s = s + _rel_bias(rel, lambda k: table_ref[:, k:k + 1])
    knew = knew_ref[0].astype(BF16).astype(F32)
    own = jnp.sum(q_f * knew, axis=1, keepdims=True) * SCALE + table_ref[:, 0:1]
    m = jnp.maximum(jnp.max(s, axis=1, keepdims=True), own)
    p = jnp.exp(s - m)
    p_own = jnp.exp(own - m)
    denom = jnp.sum(p, axis=1, keepdims=True) + p_own
    p16 = p.astype(BF16)
    head_o = lax.broadcasted_iota(I32, (N_HEADS, HEAD_DIM), 0)
    pv = jnp.zeros((N_HEADS, HEAD_DIM), F32)
    for h in range(N_HEADS):
        pv_h = lax.dot_general(p16, vbuf[slot, h].astype(BF16), _NT, preferred_element_type=F32)
        pv = jnp.where(head_o == h, pv_h, pv)
    vnew = vnew_ref[0].astype(BF16).astype(F32)
    o_ref[0] = (pv + p_own.astype(BF16).astype(F32) * vnew) / denom


def _moba_sample(pt_flat, top, rel_bias, q16, k_new, v_new, kpool, vpool, n_seq, n_pages):
    n_sel = MOBA_TOPK * MOBA_BLOCK
    head_rows = pl.BlockSpec((1, N_HEADS, HEAD_DIM), lambda b, pt, tp: (b, 0, 0))
    whole = lambda a: pl.BlockSpec(a.shape, lambda b, pt, tp: (0,) * a.ndim)
    table_t = rel_bias.T
    grid_spec = pltpu.PrefetchScalarGridSpec(
        num_scalar_prefetch=2,
        grid=(n_seq,),
        in_specs=[pl.BlockSpec((1, N_HEADS, LANES), lambda b, pt, tp: (b, 0, 0)),
                  whole(table_t),
                  head_rows, head_rows, head_rows,
                  pl.BlockSpec(memory_space=pl.ANY), pl.BlockSpec(memory_space=pl.ANY)],
        out_specs=head_rows,
        scratch_shapes=[pltpu.VMEM((SAMPLE_BUFFERS, N_HEADS, HEAD_DIM, n_sel), F32),
                        pltpu.VMEM((SAMPLE_BUFFERS, N_HEADS, HEAD_DIM, n_sel), F32),
                        pltpu.SemaphoreType.DMA((SAMPLE_BUFFERS, 2))],
    )
    return pl.pallas_call(
        functools.partial(_moba_sample_kernel, n_pages=n_pages),
        grid_spec=grid_spec,
        out_shape=jax.ShapeDtypeStruct((n_seq, N_HEADS, HEAD_DIM), F32),
        compiler_params=pltpu.CompilerParams(dimension_semantics=("arbitrary",),
                                             vmem_limit_bytes=VMEM_LIMIT),
        name="moba_sample",
    )(pt_flat, top[:, :, :MOBA_TOPK].reshape(-1), top, table_t, q16, k_new, v_new, kpool, vpool)


def kernel(x_prompt, x_sample, cache_k_sb, cache_v_sb, cache_k_moba, cache_v_moba, page_table,
           w_in, w_out, g_pre, g_post, rel_bias):
    depth, d_model, d_in = w_in.shape
    assert depth == 1 and d_in == 8 * GROUP_WIDTH and w_out.shape[1] == 2 * GROUP_WIDTH
    batch, seq, _ = x_prompt.shape
    n_seq, dec_seq, _ = x_sample.shape
    assert dec_seq == 1 and seq % MOBA_BLOCK == 0
    n_pool, page = cache_k_sb.shape[1:3]
    n_pages = page_table.shape[1]

    w_in16 = w_in[0].astype(BF16)
    w_out16 = w_out[0].astype(BF16)
    pt_flat = page_table.reshape(-1).astype(I32)
    pool4 = lambda c: jnp.transpose(c[0], (0, 2, 3, 1))
    pool3 = lambda c: pool4(c).reshape(n_pool, GROUP_WIDTH, page)

    (q_a, g_a, q_b, g_b, kt_a, vt_a, kt_b, vt_b, ka16, kb16, vta16, vtb16, kmean_b) = _project(
        x_prompt.reshape(batch * seq, d_model), g_pre, w_in16, True, seq)
    sq_a, sg_a, sq_b, sg_b, skt_a, svt_a, skt_b, svt_b, sk_b, sv_b = _project(
        x_sample.reshape(n_seq, d_model), g_pre, w_in16, False, n_seq)

    as_seq = lambda a: a.reshape(batch, seq, GROUP_WIDTH)
    y_a = _sb_prompt(as_seq(q_a), as_seq(ka16), vta16, as_seq(g_a), batch, seq)
    y_b, top = _moba_prompt(pt_flat, rel_bias, as_seq(q_b), as_seq(kb16), vtb16,
                            kmean_b.reshape(batch, seq // MOBA_BLOCK, GROUP_WIDTH),
                            _bias_tiles(rel_bias), as_seq(g_b),
                            sq_b.reshape(n_seq, 1, GROUP_WIDTH), pool3(cache_k_moba), batch, seq, n_pages)
    y_prompt = _mix_out(x_prompt.reshape(batch * seq, d_model), y_a.reshape(batch * seq, GROUP_WIDTH),
                        y_b.reshape(batch * seq, GROUP_WIDTH), None, w_out16, g_post)

    so_a = _sb_sample(pt_flat, sq_a.reshape(n_seq, 1, GROUP_WIDTH), pool3(cache_k_sb), pool3(cache_v_sb),
                      n_seq, n_pages)
    heads = lambda a: a.reshape(n_seq, N_HEADS, HEAD_DIM)
    so_b = _moba_sample(pt_flat, top, rel_bias, heads(sq_b), heads(sk_b), heads(sv_b),
                        pool4(cache_k_moba), pool4(cache_v_moba), n_seq, n_pages)
    y_sample = _mix_out(x_sample.reshape(n_seq, d_model), so_a.reshape(n_seq, GROUP_WIDTH),
                        so_b.reshape(n_seq, GROUP_WIDTH), (sg_a, sg_b), w_out16, g_post)

    kv = lambda a: jnp.transpose(a.reshape(a.shape[0], N_HEADS, HEAD_DIM, a.shape[2]), (0, 3, 1, 2))[None]
    kv_s = lambda a: jnp.transpose(kv(a), (0, 2, 1, 3, 4))
    return (y_prompt.reshape(batch, seq, d_model), y_sample.reshape(n_seq, 1, d_model),
            kv(kt_a), kv(vt_a), kv(kt_b), kv(vt_b),
            kv_s(skt_a), kv_s(svt_a), kv_s(skt_b), kv_s(svt_b))
```
